```python
import jax, jax.numpy as jnp
from jax import lax
import numpy as np


D_MODEL = 1024
BATCH = 1
SEQ = 16384
DEPTH = 2
DEC_BATCH = 8
DEC_SEQ = 2048
PAST_LEN = 128

N_MIXERS = 2
N_A_LAYERS = (DEPTH + 1) // 2
N_B_LAYERS = DEPTH // 2
D_RNN = D_MODEL
LRU_HEADS = 8
LRU_BLOCK = D_RNN // LRU_HEADS
LRU_CONV = 4
LRU_C = 8.0
ATTN_HEADS = 16
HEAD_DIM = D_MODEL // ATTN_HEADS
GRID_W = 64
WIN_ROWS = 8
WIN_COLS = 16
D_FF = 2816
FFN_CONV = 3
NORM_EPS = 1e-6

kernel_name = 'hybrid_rglru_natten_encoder'


def rms_norm(x, g):
    xf = x.astype(jnp.float32)
    y = xf * lax.rsqrt(jnp.mean(xf * xf, axis=-1, keepdims=True) + NORM_EPS)
    return (y * g.astype(jnp.float32)).astype(x.dtype)


def depthwise_conv(x, w, b, pad_left, pad_right):
    c = x.shape[-1]
    y = lax.conv_general_dilated(
        x, w[:, None, :].astype(x.dtype), window_strides=(1,),
        padding=[(pad_left, pad_right)],
        dimension_numbers=('NWC', 'WIO', 'NWC'), feature_group_count=c)
    return y + b.astype(x.dtype)


def _lin_rec_combine(c1, c2):
    a1, b1 = c1
    a2, b2 = c2
    return a1 * a2, a2 * b1 + b2


def rg_lru_direction(xc, gate_w, gate_b, lam, reverse):
    bsz, s, _ = xc.shape
    xh = xc.reshape(bsz, s, LRU_HEADS, LRU_BLOCK)
    gates = jnp.einsum('bshi,ghij->gbshj', xh, gate_w.astype(xc.dtype)) + gate_b[:, None, None].astype(xc.dtype)
    gates = jax.nn.sigmoid(gates.astype(jnp.float32)).reshape(2, bsz, s, D_RNN)
    r, i = gates[0], gates[1]
    log_a = -LRU_C * r * jax.nn.softplus(-lam.astype(jnp.float32))
    a = jnp.exp(log_a)
    mult = jnp.sqrt(-jnp.expm1(2.0 * log_a))
    b = mult * i * xc.astype(jnp.float32)
    _, h = lax.associative_scan(_lin_rec_combine, (a, b), reverse=reverse, axis=1)
    return h


def rglru_mixer(x, w_in, conv_w, conv_b, gate_w, gate_b, lam, w_out):
    proj = x @ w_in.astype(x.dtype)
    gate_branch, rec_branch = jnp.split(proj, 2, axis=-1)
    xc = depthwise_conv(rec_branch, conv_w, conv_b, LRU_CONV // 2, LRU_CONV - 1 - LRU_CONV // 2)
    h = (rg_lru_direction(xc, gate_w[0], gate_b[0], lam[0], False)
         + rg_lru_direction(xc, gate_w[1], gate_b[1], lam[1], True))
    y = h.astype(x.dtype) * jax.nn.gelu(gate_branch, approximate=True)
    return y @ w_out.astype(x.dtype)


def neighbourhood_attention(x, w_qkv, rpb, w_o):
    bsz, s, _ = x.shape
    rows = s // GRID_W
    kr = min(WIN_ROWS, rows)
    qkv = (x @ w_qkv.astype(x.dtype)).reshape(bsz, rows, GRID_W, 3, ATTN_HEADS, HEAD_DIM)
    q = qkv[:, :, :, 0] * (HEAD_DIM ** -0.5)
    k = qkv[:, :, :, 1]
    v = qkv[:, :, :, 2]
    row_ids = jnp.arange(rows, dtype=jnp.int32)
    row_start = jnp.clip(row_ids - kr // 2, 0, rows - kr)
    col_ids = jnp.arange(GRID_W, dtype=jnp.int32)
    col_start = jnp.clip(col_ids - WIN_COLS // 2, 0, GRID_W - WIN_COLS)
    col_idx = col_start[:, None] + jnp.arange(WIN_COLS, dtype=jnp.int32)
    dc = col_idx - col_ids[:, None] + (WIN_COLS - 1)

    def one_row(args):
        q_r, r, rs = args
        k_rows = lax.dynamic_slice_in_dim(k, rs, kr, axis=1)
        v_rows = lax.dynamic_slice_in_dim(v, rs, kr, axis=1)
        k_win = k_rows[:, :, col_idx]
        v_win = v_rows[:, :, col_idx]
        dr = rs + jnp.arange(kr, dtype=jnp.int32) - r + (WIN_ROWS - 1)
        bias = rpb[:, dr[None, :, None], dc[:, None, :]]
        scores = (jnp.einsum('bqhd,brqchd->bhqrc', q_r, k_win).astype(jnp.float32)
                  + bias.astype(jnp.float32)[None])
        p = jax.nn.softmax(scores.reshape(bsz, ATTN_HEADS, GRID_W, kr * WIN_COLS), axis=-1)
        p = p.reshape(scores.shape).astype(v.dtype)
        return jnp.einsum('bhqrc,brqchd->bqhd', p, v_win)

    out = lax.map(one_row, (jnp.moveaxis(q, 1, 0), row_ids, row_start))
    out = jnp.moveaxis(out, 0, 1).reshape(bsz, s, D_MODEL)
    return out @ w_o.astype(x.dtype)


def conv_ffn(x, w_up, conv_w, conv_b, w_down):
    h = x @ w_up.astype(x.dtype)
    h = depthwise_conv(h, conv_w, conv_b, FFN_CONV // 2, FFN_CONV // 2)
    g, u = jnp.split(h, 2, axis=-1)
    return (jax.nn.gelu(g, approximate=True) * u) @ w_down.astype(x.dtype)


def trunk(x, norm_mix, norm_ffn, norm_final, lru_w_in, lru_conv_w, lru_conv_b,
          lru_gate_w, lru_gate_b, lru_lambda, lru_w_out, attn_w_qkv, attn_rpb,
          attn_w_o, ffn_w_up, ffn_conv_w, ffn_conv_b, ffn_w_down):
    for i in range(DEPTH):
        j = i // N_MIXERS
        h = rms_norm(x, norm_mix[i])
        if i % N_MIXERS == 0:
            h = rglru_mixer(h, lru_w_in[j], lru_conv_w[j], lru_conv_b[j],
                            lru_gate_w[j], lru_gate_b[j], lru_lambda[j], lru_w_out[j])
        else:
            h = neighbourhood_attention(h, attn_w_qkv[j], attn_rpb[j], attn_w_o[j])
        x = x + h
        x = x + conv_ffn(rms_norm(x, norm_ffn[i]), ffn_w_up[i], ffn_conv_w[i],
                         ffn_conv_b[i], ffn_w_down[i])
    return rms_norm(x, norm_final)


def setup_inputs(seed: int = 0) -> dict:
    key = jax.random.key(seed)
    ks = jax.random.split(key, 20)

    def nrm(k, shape, scale):
        return jax.random.normal(k, shape, jnp.float32) * scale

    x_prompt = nrm(ks[0], (BATCH, SEQ, D_MODEL), 1.0)
    x_sample = nrm(ks[1], (DEC_BATCH, DEC_SEQ, D_MODEL), 1.0)
    norm_mix = 1.0 + nrm(ks[2], (DEPTH, D_MODEL), 0.02)
    norm_ffn = 1.0 + nrm(ks[3], (DEPTH, D_MODEL), 0.02)
    norm_final = 1.0 + nrm(ks[4], (D_MODEL,), 0.02)
    lru_w_in = nrm(ks[5], (N_A_LAYERS, D_MODEL, 2 * D_RNN), D_MODEL ** -0.5)
    lru_conv_w = nrm(ks[6], (N_A_LAYERS, LRU_CONV, D_RNN), LRU_CONV ** -0.5)
    lru_conv_b = nrm(ks[7], (N_A_LAYERS, D_RNN), 0.01)
    lru_gate_w = nrm(ks[8], (N_A_LAYERS, 2, 2, LRU_HEADS, LRU_BLOCK, LRU_BLOCK), LRU_BLOCK ** -0.5)
    lru_gate_b = nrm(ks[9], (N_A_LAYERS, 2, 2, LRU_HEADS, LRU_BLOCK), 0.01)
    a_target = jax.random.uniform(ks[10], (N_A_LAYERS, 2, D_RNN), jnp.float32, 0.9, 0.999)
    s = a_target ** (1.0 / LRU_C)
    lru_lambda = jnp.log(s) - jnp.log1p(-s)
    lru_w_out = nrm(ks[11], (N_A_LAYERS, D_RNN, D_MODEL), D_RNN ** -0.5)
    attn_w_qkv = nrm(ks[12], (N_B_LAYERS, D_MODEL, 3 * D_MODEL), D_MODEL ** -0.5)
    attn_rpb = nrm(ks[13], (N_B_LAYERS, ATTN_HEADS, 2 * WIN_ROWS - 1, 2 * WIN_COLS - 1), 0.1)
    attn_w_o = nrm(ks[14], (N_B_LAYERS, D_MODEL, D_MODEL), D_MODEL ** -0.5)
    ffn_w_up = nrm(ks[15], (DEPTH, D_MODEL, 2 * D_FF), D_MODEL ** -0.5)
    ffn_conv_w = nrm(ks[16], (DEPTH, FFN_CONV, 2 * D_FF), FFN_CONV ** -0.5)
    ffn_conv_b = nrm(ks[17], (DEPTH, 2 * D_FF), 0.01)
    ffn_w_down = nrm(ks[18], (DEPTH, D_FF, D_MODEL), D_FF ** -0.5)
    return {'x_prompt': x_prompt, 'x_sample': x_sample, 'norm_mix': norm_mix,
            'norm_ffn': norm_ffn, 'norm_final': norm_final, 'lru_w_in': lru_w_in,
            'lru_conv_w': lru_conv_w, 'lru_conv_b': lru_conv_b, 'lru_gate_w': lru_gate_w,
            'lru_gate_b': lru_gate_b, 'lru_lambda': lru_lambda, 'lru_w_out': lru_w_out,
            'attn_w_qkv': attn_w_qkv, 'attn_rpb': attn_rpb, 'attn_w_o': attn_w_o,
            'ffn_w_up': ffn_w_up, 'ffn_conv_w': ffn_conv_w, 'ffn_conv_b': ffn_conv_b,
            'ffn_w_down': ffn_w_down}


def reference(x_prompt, x_sample, norm_mix, norm_ffn, norm_final, lru_w_in, lru_conv_w,
              lru_conv_b, lru_gate_w, lru_gate_b, lru_lambda, lru_w_out, attn_w_qkv,
              attn_rpb, attn_w_o, ffn_w_up, ffn_conv_w, ffn_conv_b, ffn_w_down):
    y_prompt = trunk(x_prompt, norm_mix, norm_ffn, norm_final, lru_w_in, lru_conv_w,
                     lru_conv_b, lru_gate_w, lru_gate_b, lru_lambda, lru_w_out,
                     attn_w_qkv, attn_rpb, attn_w_o, ffn_w_up, ffn_conv_w, ffn_conv_b,
                     ffn_w_down)
    y_sample = trunk(x_sample, norm_mix, norm_ffn, norm_final, lru_w_in, lru_conv_w,
                     lru_conv_b, lru_gate_w, lru_gate_b, lru_lambda, lru_w_out,
                     attn_w_qkv, attn_rpb, attn_w_o, ffn_w_up, ffn_conv_w, ffn_conv_b,
                     ffn_w_down)
    return (y_prompt, y_sample)
```

```python
import functools

import jax
import jax.numpy as jnp
from jax import lax
from jax.experimental import pallas as pl
from jax.experimental.pallas import tpu as pltpu

D_MODEL = 1024
LRU_HEADS = 8
LRU_BLOCK = 128
LRU_CONV = 4
LRU_C = 8.0
ATTN_HEADS = 16
HEAD_DIM = 64
GRID_W = 64
WIN_ROWS = 8
WIN_COLS = 16
D_FF = 2816
NORM_EPS = 1e-6

LANES = 128
SUBLANES = 8
TOKEN_TILE = 512
HALO = SUBLANES
SEG = TOKEN_TILE // SUBLANES
FF_CHUNK = 256
ROW_BLOCK = TOKEN_TILE // GRID_W
VMEM_LIMIT = 56 * 1024 * 1024
NEG_BIG = -1e30

_BF16 = jnp.bfloat16
_F32 = jnp.float32


def _const_spec(shape):
    nd = len(shape)
    return pl.BlockSpec(shape, lambda i: (0,) * nd, pipeline_mode=pl.Buffered(1))


def _rms_norm(x, g):
    ms = jnp.mean(x * x, axis=-1, keepdims=True)
    return x * lax.rsqrt(ms + NORM_EPS) * g


def _gelu_tanh(x):
    inner = 0.7978845608028654 * (x + 0.044715 * (x * x * x))
    return 0.5 * x * (1.0 + jnp.tanh(inner))


def _sigmoid(x):
    return 0.5 * jnp.tanh(0.5 * x) + 0.5


def _softplus(z):
    e = jnp.exp(-jnp.abs(z))
    u = 1.0 + e
    log1p = jnp.where(u == 1.0, e, jnp.log(u) * (e / (u - 1.0)))
    return jnp.maximum(z, 0.0) + log1p


def _halo_tile(x_ref, xp_ref, xn_ref, tiles_per_seq, tile):
    t_in_seq = tile % tiles_per_seq
    keep_prev = jnp.where(t_in_seq == 0, 0.0, 1.0)
    keep_next = jnp.where(t_in_seq == tiles_per_seq - 1, 0.0, 1.0)
    return jnp.concatenate(
        [xp_ref[...] * keep_prev, x_ref[...], xn_ref[...] * keep_next], axis=0)


def _halo_specs(n_tokens):
    per = TOKEN_TILE // HALO
    last = n_tokens // HALO - 1
    return [
        pl.BlockSpec((TOKEN_TILE, D_MODEL), lambda i: (i, 0)),
        pl.BlockSpec((HALO, D_MODEL), lambda i: (jnp.maximum(i * per - 1, 0), 0)),
        pl.BlockSpec((HALO, D_MODEL), lambda i: (jnp.minimum((i + 1) * per, last), 0)),
    ]


def _to_scan_order(dst_ref, val):
    for j in range(SUBLANES):
        dst_ref[pl.ds(j, SEG, stride=SUBLANES), :] = val[j * SEG:(j + 1) * SEG, :]


def _segment_rows(src_ref, j):
    return src_ref[pl.ds(j, SEG, stride=SUBLANES), :]


def _local_scan(sa_ref, sb_ref, hl_ref, ac_ref, reverse):
    def body(k, carry):
        h, acum = carry
        s = (SEG - 1 - k) if reverse else k
        row = pl.multiple_of(s * SUBLANES, SUBLANES)
        a = sa_ref[pl.ds(row, SUBLANES), :]
        b = sb_ref[pl.ds(row, SUBLANES), :]
        h = a * h + b
        acum = a * acum
        hl_ref[pl.ds(row, SUBLANES), :] = h
        ac_ref[pl.ds(row, SUBLANES), :] = acum
        return h, acum

    init = (jnp.zeros((SUBLANES, LANES), _F32), jnp.ones((SUBLANES, LANES), _F32))
    return lax.fori_loop(0, SEG, body, init, unroll=8)


def _chain_segments(h_end, a_end, carry_in, reverse):
    order = range(SUBLANES - 1, -1, -1) if reverse else range(SUBLANES)
    states = [None] * SUBLANES
    cur = carry_in
    for j in order:
        states[j] = cur
        cur = h_end[j:j + 1, :] + a_end[j:j + 1, :] * cur
    return states, cur


def _scan_tile(a, b, sa_ref, sb_ref, hl_ref, ac_ref, carry_in, reverse):
    _to_scan_order(sa_ref, a)
    _to_scan_order(sb_ref, b)
    h_end, a_end = _local_scan(sa_ref, sb_ref, hl_ref, ac_ref, reverse)
    states, carry_out = _chain_segments(h_end, a_end, carry_in, reverse)
    segs = [_segment_rows(hl_ref, j) + _segment_rows(ac_ref, j) * states[j]
            for j in range(SUBLANES)]
    return segs, carry_out


def _lru_in_kernel(x_ref, xp_ref, xn_ref, g_ref, win_ref, cw_ref, cb_ref, gw_ref, gb_ref,
                   lam_ref, gate_out, hf_out, ab_out, bb_out,
                   rec_scr, xc_scr, sa_scr, sb_scr, hl_scr, ac_scr, carry_scr,
                   *, tiles_per_seq):
    tile = pl.program_id(0)
    t = TOKEN_TILE
    xa = _halo_tile(x_ref, xp_ref, xn_ref, tiles_per_seq, tile)
    xan = _rms_norm(xa, g_ref[...]).astype(_BF16)
    proj = jnp.dot(xan, win_ref[...], preferred_element_type=_F32)
    gate_out[...] = _gelu_tanh(proj[HALO:HALO + t, :D_MODEL])
    for c in range(LRU_HEADS):
        rec_scr[c] = proj[:, D_MODEL + c * LANES:D_MODEL + (c + 1) * LANES]

    @pl.when(tile % tiles_per_seq == 0)
    def _():
        carry_scr[...] = jnp.zeros_like(carry_scr)

    decay = -LRU_C * _softplus(-lam_ref[...])

    for c in range(LRU_HEADS):
        lanes = slice(c * LANES, (c + 1) * LANES)
        cw = cw_ref[:, lanes]
        cb = cb_ref[:, lanes]
        for s in range(SUBLANES):
            acc = cb
            for k in range(LRU_CONV):
                win = rec_scr[c, pl.ds(HALO - 2 + s + k, t // SUBLANES, stride=SUBLANES), :]
                acc = acc + win * cw[k:k + 1, :]
            xc_scr[pl.ds(s, t // SUBLANES, stride=SUBLANES), :] = acc
        xc = xc_scr[...]
        gates = jnp.dot(xc.astype(_BF16), gw_ref[c], preferred_element_type=_F32) + gb_ref[c]
        ab = []
        for d in range(2):
            r = _sigmoid(gates[:, (2 * d) * LANES:(2 * d + 1) * LANES])
            i_gate = _sigmoid(gates[:, (2 * d + 1) * LANES:(2 * d + 2) * LANES])
            a = jnp.exp(r * decay[d:d + 1, lanes])
            mult = jnp.sqrt(1.0 - a * a)
            ab.append((a, mult * i_gate * xc))
        ab_out[:, lanes] = ab[1][0]
        bb_out[:, lanes] = ab[1][1]
        segs, carry = _scan_tile(ab[0][0], ab[0][1], sa_scr, sb_scr, hl_scr, ac_scr,
                                 carry_scr[c:c + 1, :], reverse=False)
        carry_scr[c:c + 1, :] = carry
        for j in range(SUBLANES):
            hf_out[j * SEG:(j + 1) * SEG, lanes] = segs[j]


def _lru_in(x, g, w_in, conv_w, conv_b, gate_w, gate_b, lam, seq_len):
    n = x.shape[0]
    nt = n // TOKEN_TILE
    t = TOKEN_TILE
    tok = pl.BlockSpec((t, D_MODEL), lambda i: (i, 0))
    out = jax.ShapeDtypeStruct((n, D_MODEL), _F32)
    return pl.pallas_call(
        functools.partial(_lru_in_kernel, tiles_per_seq=seq_len // t),
        grid=(nt,),
        in_specs=_halo_specs(n) + [
            _const_spec((1, D_MODEL)),
            _const_spec((D_MODEL, 2 * D_MODEL)),
            _const_spec((LRU_CONV, D_MODEL)),
            _const_spec((1, D_MODEL)),
            _const_spec((LRU_HEADS, LRU_BLOCK, 4 * LRU_BLOCK)),
            _const_spec((LRU_HEADS, 1, 4 * LRU_BLOCK)),
            _const_spec((2, D_MODEL)),
        ],
        out_specs=[tok, tok, tok, tok],
        out_shape=[out, out, out, out],
        scratch_shapes=[
            pltpu.VMEM((LRU_HEADS, t + 2 * HALO, LANES), _F32),
            pltpu.VMEM((t, LANES), _F32),
            pltpu.VMEM((t, LANES), _F32),
            pltpu.VMEM((t, LANES), _F32),
            pltpu.VMEM((t, LANES), _F32),
            pltpu.VMEM((t, LANES), _F32),
            pltpu.VMEM((LRU_HEADS, LANES), _F32),
        ],
        compiler_params=pltpu.CompilerParams(
            dimension_semantics=("arbitrary",), vmem_limit_bytes=VMEM_LIMIT),
        name="lru_in",
    )(x, x, x, g, w_in, conv_w, conv_b, gate_w, gate_b, lam)


def _lru_out_kernel(x_ref, gate_ref, hf_ref, ab_ref, bb_ref, wout_ref, out_ref,
                    y_scr, sa_scr, sb_scr, hl_scr, ac_scr, carry_scr,
                    *, tiles_per_seq, n_tiles):
    tile = n_tiles - 1 - pl.program_id(0)

    @pl.when(tile % tiles_per_seq == tiles_per_seq - 1)
    def _():
        carry_scr[...] = jnp.zeros_like(carry_scr)

    for c in range(LRU_HEADS):
        lanes = slice(c * LANES, (c + 1) * LANES)
        segs, carry = _scan_tile(ab_ref[:, lanes], bb_ref[:, lanes], sa_scr, sb_scr,
                                 hl_scr, ac_scr, carry_scr[c:c + 1, :], reverse=True)
        carry_scr[c:c + 1, :] = carry
        for j in range(SUBLANES):
            rows = slice(j * SEG, (j + 1) * SEG)
            h = hf_ref[rows, lanes] + segs[j]
            y_scr[rows, lanes] = (h * gate_ref[rows, lanes]).astype(_BF16)
    out_ref[...] = x_ref[...] + jnp.dot(y_scr[...], wout_ref[...],
                                        preferred_element_type=_F32)


def _lru_out(x, gate, hf, ab, bb, w_out, seq_len):
    n = x.shape[0]
    nt = n // TOKEN_TILE
    t = TOKEN_TILE
    tok = pl.BlockSpec((t, D_MODEL), lambda i: (nt - 1 - i, 0))
    return pl.pallas_call(
        functools.partial(_lru_out_kernel, tiles_per_seq=seq_len // t, n_tiles=nt),
        grid=(nt,),
        in_specs=[tok, tok, tok, tok, tok, _const_spec((D_MODEL, D_MODEL))],
        out_specs=tok,
        out_shape=jax.ShapeDtypeStruct((n, D_MODEL), _F32),
        scratch_shapes=[
            pltpu.VMEM((t, D_MODEL), _BF16),
            pltpu.VMEM((t, LANES), _F32),
            pltpu.VMEM((t, LANES), _F32),
            pltpu.VMEM((t, LANES), _F32),
            pltpu.VMEM((t, LANES), _F32),
            pltpu.VMEM((LRU_HEADS, LANES), _F32),
        ],
        compiler_params=pltpu.CompilerParams(
            dimension_semantics=("arbitrary",), vmem_limit_bytes=VMEM_LIMIT),
        name="lru_out",
    )(x, gate, hf, ab, bb, w_out)


def _ffn_kernel(x_ref, xp_ref, xn_ref, g_ref, wup_ref, cw_ref, cb_ref, wdn_ref, gfin_ref,
                out_ref, h_scr, acc_scr, *, tiles_per_seq, final_norm):
    tile = pl.program_id(0)
    t = TOKEN_TILE
    nblk = FF_CHUNK // LANES
    xa = _halo_tile(x_ref, xp_ref, xn_ref, tiles_per_seq, tile)
    xan = _rms_norm(xa, g_ref[...]).astype(_BF16)
    acc = jnp.zeros((t, D_MODEL), _F32)
    for ck in range(D_FF // FF_CHUNK):
        for half in range(2):
            col0 = half * D_FF + ck * FF_CHUNK
            h = jnp.dot(xan, wup_ref[:, col0:col0 + FF_CHUNK], preferred_element_type=_F32)
            for b in range(nblk):
                h_scr[half * nblk + b] = h[:, b * LANES:(b + 1) * LANES]
        blocks = []
        for b in range(nblk):
            pieces = []
            for s in range(SUBLANES):
                conv = []
                for half in range(2):
                    col = half * D_FF + ck * FF_CHUNK + b * LANES
                    cw = cw_ref[:, col:col + LANES]
                    val = cb_ref[:, col:col + LANES]
                    for k in range(3):
                        win = h_scr[half * nblk + b,
                                    pl.ds(HALO - 1 + s + k, t // SUBLANES, stride=SUBLANES), :]
                        val = val + win * cw[k:k + 1, :]
                    conv.append(val)
                pieces.append(_gelu_tanh(conv[0]) * conv[1])
            blocks.append(jnp.concatenate(pieces, axis=0))
        p = jnp.concatenate(blocks, axis=1).astype(_BF16)
        acc = acc + jnp.dot(p, wdn_ref[ck * FF_CHUNK:(ck + 1) * FF_CHUNK, :],
                            preferred_element_type=_F32)
    for c in range(D_MODEL // LANES):
        for s in range(SUBLANES):
            acc_scr[c, pl.ds(s, t // SUBLANES, stride=SUBLANES), :] = (
                acc[s * (t // SUBLANES):(s + 1) * (t // SUBLANES), c * LANES:(c + 1) * LANES])
    y = x_ref[...] + jnp.concatenate(
        [acc_scr[c] for c in range(D_MODEL // LANES)], axis=1)
    if final_norm:
        y = _rms_norm(y, gfin_ref[...])
    out_ref[...] = y


def _conv_ffn(x, g, w_up, conv_w, conv_b, w_down, g_final, seq_len, final_norm):
    n = x.shape[0]
    t = TOKEN_TILE
    return pl.pallas_call(
        functools.partial(_ffn_kernel, tiles_per_seq=seq_len // t, final_norm=final_norm),
        grid=(n // t,),
        in_specs=_halo_specs(n) + [
            _const_spec((1, D_MODEL)),
            _const_spec((D_MODEL, 2 * D_FF)),
            _const_spec((3, 2 * D_FF)),
            _const_spec((1, 2 * D_FF)),
            _const_spec((D_FF, D_MODEL)),
            _const_spec((1, D_MODEL)),
        ],
        out_specs=pl.BlockSpec((t, D_MODEL), lambda i: (i, 0)),
        out_shape=jax.ShapeDtypeStruct((n, D_MODEL), _F32),
        scratch_shapes=[
            pltpu.VMEM((2 * FF_CHUNK // LANES, t + 2 * HALO, LANES), _F32),
            pltpu.VMEM((D_MODEL // LANES, t, LANES), _F32),
        ],
        compiler_params=pltpu.CompilerParams(
            dimension_semantics=("arbitrary",), vmem_limit_bytes=VMEM_LIMIT),
        name="conv_ffn",
    )(x, x, x, g, w_up, conv_w, conv_b, w_down, g_final)


def _qkv_kernel(x_ref, g_ref, w_ref, q_ref, k_ref, v_ref):
    xn = _rms_norm(x_ref[...], g_ref[...]).astype(_BF16)
    qkv = jnp.dot(xn, w_ref[...], preferred_element_type=_F32)
    q_ref[...] = (qkv[:, :D_MODEL] * (HEAD_DIM ** -0.5)).astype(_BF16)
    k_ref[...] = qkv[:, D_MODEL:2 * D_MODEL].astype(_BF16)
    v_ref[...] = qkv[:, 2 * D_MODEL:].astype(_BF16)


def _qkv(x, g, w_qkv):
    n = x.shape[0]
    t = TOKEN_TILE
    tok = pl.BlockSpec((t, D_MODEL), lambda i: (i, 0))
    out = jax.ShapeDtypeStruct((n, D_MODEL), _BF16)
    return pl.pallas_call(
        _qkv_kernel,
        grid=(n // t,),
        in_specs=[tok, _const_spec((1, D_MODEL)), _const_spec((D_MODEL, 3 * D_MODEL))],
        out_specs=[tok, tok, tok],
        out_shape=[out, out, out],
        compiler_params=pltpu.CompilerParams(
            dimension_semantics=("arbitrary",), vmem_limit_bytes=VMEM_LIMIT),
        name="qkv",
    )(x, g, w_qkv)


def _attn_kernel(q_ref, kp_ref, kc_ref, kn_ref, vp_ref, vc_ref, vn_ref, bias_ref,
                 x_ref, wo_ref, out_ref, kcat, vcat, o_scr, *, blocks_per_seq):
    t = TOKEN_TILE
    win = WIN_ROWS * GRID_W
    rows_per_seq = blocks_per_seq * ROW_BLOCK
    blk = pl.program_id(0) % blocks_per_seq
    kcat[0:t] = kp_ref[...]
    kcat[t:2 * t] = kc_ref[...]
    kcat[2 * t:3 * t] = kn_ref[...]
    vcat[0:t] = vp_ref[...]
    vcat[t:2 * t] = vc_ref[...]
    vcat[2 * t:3 * t] = vn_ref[...]
    lane = lax.broadcasted_iota(jnp.int32, (GRID_W, LANES), 1)
    low = lane < HEAD_DIM

    def row_body(qr, carry):
        r = blk * ROW_BLOCK + qr
        rs = jnp.clip(r - WIN_ROWS // 2, 0, rows_per_seq - WIN_ROWS)
        shift = (WIN_ROWS - 1) - (r - rs)
        par = shift % 2
        m0 = shift // 2
        kstart = pl.multiple_of((rs - (blk - 1) * ROW_BLOCK) * GRID_W, GRID_W)
        qrow = pl.multiple_of(qr * GRID_W, GRID_W)
        for hp in range(ATTN_HEADS // 2):
            lanes = slice(hp * LANES, (hp + 1) * LANES)
            q2 = q_ref[pl.ds(qrow, GRID_W), lanes]
            zero = jnp.zeros_like(q2)
            qs = jnp.concatenate([jnp.where(low, q2, zero), jnp.where(low, zero, q2)], axis=0)
            k2 = kcat[pl.ds(kstart, win), lanes]
            s = lax.dot_general(qs, k2, (((1,), (1,)), ((), ())),
                                preferred_element_type=_F32)
            bias = jnp.concatenate(
                [jnp.concatenate([bias_ref[par, 2 * hp + hh, m0 + m] for m in range(4)], axis=1)
                 for hh in range(2)], axis=0)
            s = s + bias
            mx = jnp.max(s, axis=-1, keepdims=True)
            p = jnp.exp(s - mx)
            denom = jnp.sum(p, axis=-1, keepdims=True)
            v2 = vcat[pl.ds(kstart, win), lanes]
            o = jnp.dot(p.astype(_BF16), v2, preferred_element_type=_F32)
            o = o * (1.0 / denom)
            o2 = jnp.where(low, o[:GRID_W], o[GRID_W:])
            o_scr[pl.ds(qrow, GRID_W), lanes] = o2.astype(_BF16)
        return carry

    lax.fori_loop(0, ROW_BLOCK, row_body, 0)
    out_ref[...] = x_ref[...] + jnp.dot(o_scr[...], wo_ref[...], preferred_element_type=_F32)


def _attention(x, q, k, v, bias_tab, w_o, seq_len):
    n = x.shape[0]
    t = TOKEN_TILE
    nt = n // t
    tok = pl.BlockSpec((t, D_MODEL), lambda i: (i, 0))
    prev = pl.BlockSpec((t, D_MODEL), lambda i: (jnp.maximum(i - 1, 0), 0))
    nxt = pl.BlockSpec((t, D_MODEL), lambda i: (jnp.minimum(i + 1, nt - 1), 0))
    return pl.pallas_call(
        functools.partial(_attn_kernel, blocks_per_seq=seq_len // t),
        grid=(nt,),
        in_specs=[tok, prev, tok, nxt, prev, tok, nxt,
                  _const_spec(bias_tab.shape), tok, _const_spec((D_MODEL, D_MODEL))],
        out_specs=tok,
        out_shape=jax.ShapeDtypeStruct((n, D_MODEL), _F32),
        scratch_shapes=[
            pltpu.VMEM((3 * t, D_MODEL), _BF16),
            pltpu.VMEM((3 * t, D_MODEL), _BF16),
            pltpu.VMEM((t, D_MODEL), _BF16),
        ],
        compiler_params=pltpu.CompilerParams(
            dimension_semantics=("arbitrary",), vmem_limit_bytes=VMEM_LIMIT),
        name="attention",
    )(q, k, k, k, v, v, v, bias_tab, x, w_o)


def _bias_table(rpb):
    col = jnp.arange(GRID_W, dtype=jnp.int32)
    col_start = jnp.clip(col - WIN_COLS // 2, 0, GRID_W - WIN_COLS)
    kc = jnp.arange(GRID_W, dtype=jnp.int32)
    valid = (kc[None, :] >= col_start[:, None]) & (kc[None, :] < col_start[:, None] + WIN_COLS)
    dc = jnp.clip(kc[None, :] - col[:, None] + (WIN_COLS - 1), 0, 2 * WIN_COLS - 2)
    full = rpb.astype(_F32)[:, :, dc]
    full = jnp.where(valid[None, None], full, NEG_BIG)
    flat = jnp.transpose(full, (0, 2, 1, 3)).reshape(ATTN_HEADS, GRID_W, -1)
    flat = jnp.pad(flat, ((0, 0), (0, 0), (0, 64)), constant_values=NEG_BIG)
    tabs = []
    for par in range(2):
        sl = flat[:, :, 64 * par:64 * par + 7 * LANES]
        tabs.append(jnp.transpose(sl.reshape(ATTN_HEADS, GRID_W, 7, LANES), (0, 2, 1, 3)))
    return jnp.stack(tabs, axis=0)


def _trunk(x, seq_len, p):
    gate, hf, ab, bb = _lru_in(x, p["norm_mix"][0], p["w_in"], p["lru_conv_w"], p["lru_conv_b"],
                               p["gate_w"], p["gate_b"], p["lam"], seq_len)
    x = _lru_out(x, gate, hf, ab, bb, p["w_out"], seq_len)
    x = _conv_ffn(x, p["norm_ffn"][0], p["w_up"][0], p["ffn_conv_w"][0], p["ffn_conv_b"][0],
                  p["w_down"][0], p["norm_final"], seq_len, final_norm=False)
    q, k, v = _qkv(x, p["norm_mix"][1], p["w_qkv"])
    x = _attention(x, q, k, v, p["bias_tab"], p["w_o"], seq_len)
    x = _conv_ffn(x, p["norm_ffn"][1], p["w_up"][1], p["ffn_conv_w"][1], p["ffn_conv_b"][1],
                  p["w_down"][1], p["norm_final"], seq_len, final_norm=True)
    return x


def kernel(x_prompt, x_sample, norm_mix, norm_ffn, norm_final, lru_w_in, lru_conv_w, lru_conv_b, lru_gate_w, lru_gate_b, lru_lambda, lru_w_out, attn_w_qkv, attn_rpb, attn_w_o, ffn_w_up, ffn_conv_w, ffn_conv_b, ffn_w_down):
    gw = jnp.transpose(lru_gate_w[0], (2, 3, 0, 1, 4)).reshape(
        LRU_HEADS, LRU_BLOCK, 4 * LRU_BLOCK).astype(_BF16)
    gb = jnp.transpose(lru_gate_b[0], (2, 0, 1, 3)).reshape(LRU_HEADS, 1, 4 * LRU_BLOCK)
    p = {
        "norm_mix": [norm_mix[i][None, :] for i in range(2)],
        "norm_ffn": [norm_ffn[i][None, :] for i in range(2)],
        "norm_final": norm_final[None, :],
        "w_in": lru_w_in[0].astype(_BF16),
        "lru_conv_w": lru_conv_w[0],
        "lru_conv_b": lru_conv_b[0][None, :],
        "gate_w": gw,
        "gate_b": gb,
        "lam": lru_lambda[0],
        "w_out": lru_w_out[0].astype(_BF16),
        "w_qkv": attn_w_qkv[0].astype(_BF16),
        "bias_tab": _bias_table(attn_rpb[0]),
        "w_o": attn_w_o[0].astype(_BF16),
        "w_up": [ffn_w_up[i].astype(_BF16) for i in range(2)],
        "ffn_conv_w": [ffn_conv_w[i] for i in range(2)],
        "ffn_conv_b": [ffn_conv_b[i][None, :] for i in range(2)],
        "w_down": [ffn_w_down[i].astype(_BF16) for i in range(2)],
    }
    outs = []
    for x in (x_prompt, x_sample):
        b, seq_len, d = x.shape
        y = _trunk(x.reshape(b * seq_len, d), seq_len, p)
        outs.append(y.reshape(b, seq_len, d))
    return tuple(outs)
```

```python
import functools

import jax
import jax.numpy as jnp
from jax import lax
from jax.experimental import pallas as pl
from jax.experimental.pallas import tpu as pltpu

D_MODEL = 1024
LRU_HEADS = 8
LRU_BLOCK = 128
LRU_CONV = 4
LRU_C = 8.0
ATTN_HEADS = 16
HEAD_DIM = 64
GRID_W = 64
WIN_ROWS = 8
WIN_COLS = 16
D_FF = 2816
NORM_EPS = 1e-6

LANES = 128
SUBLANES = 8
TOKEN_TILE = 512
HALO = SUBLANES
SEG = TOKEN_TILE // SUBLANES
FF_CHUNK = 256
CONV_STRIDE = 4
ROW_BLOCK = TOKEN_TILE // GRID_W
VMEM_LIMIT = 56 * 1024 * 1024
NEG_BIG = -1e30
TINY = 1e-30
LOG2_E = 1.4426950408889634

_BF16 = jnp.bfloat16
_F32 = jnp.float32


def _const_spec(shape):
    nd = len(shape)
    return pl.BlockSpec(shape, lambda i: (0,) * nd, pipeline_mode=pl.Buffered(1))


def _rms_norm(x, g):
    ms = jnp.mean(x * x, axis=-1, keepdims=True)
    return x * lax.rsqrt(ms + NORM_EPS) * g


def _gelu_tanh(x):
    inner = 0.7978845608028654 * (x + 0.044715 * (x * x * x))
    return 0.5 * x * (1.0 + jnp.tanh(inner))


def _softplus(z):
    e = jnp.exp(-jnp.abs(z))
    u = 1.0 + e
    log1p = jnp.where(u == 1.0, e, jnp.log(u) * (e / (u - 1.0)))
    return jnp.maximum(z, 0.0) + log1p


def _halo_tile(x_ref, xp_ref, xn_ref, tiles_per_seq, tile):
    t_in_seq = tile % tiles_per_seq
    keep_prev = jnp.where(t_in_seq == 0, 0.0, 1.0)
    keep_next = jnp.where(t_in_seq == tiles_per_seq - 1, 0.0, 1.0)
    return jnp.concatenate(
        [x_ref[...], xp_ref[...] * keep_prev, xn_ref[...] * keep_next], axis=0)


def _store_time_ordered(dst_ref, slot, val):
    t = TOKEN_TILE
    dst_ref[slot, HALO:HALO + t, :] = val[:t]
    dst_ref[slot, 0:HALO, :] = val[t:t + HALO]
    dst_ref[slot, HALO + t:, :] = val[t + HALO:]


def _halo_specs(n_tokens):
    per = TOKEN_TILE // HALO
    last = n_tokens // HALO - 1
    return [
        pl.BlockSpec((TOKEN_TILE, D_MODEL), lambda i: (i, 0)),
        pl.BlockSpec((HALO, D_MODEL), lambda i: (jnp.maximum(i * per - 1, 0), 0)),
        pl.BlockSpec((HALO, D_MODEL), lambda i: (jnp.minimum((i + 1) * per, last), 0)),
    ]


def _to_scan_order(dst_ref, c, val):
    for j in range(SUBLANES):
        dst_ref[c, pl.ds(j, SEG, stride=SUBLANES), :] = val[j * SEG:(j + 1) * SEG, :]


def _segment_rows(src_ref, c, j):
    return src_ref[c, pl.ds(j, SEG, stride=SUBLANES), :]


def _local_scan(sa_ref, sb_ref, hl_ref, ac_ref, reverse):
    def body(k, carry):
        s = (SEG - 1 - k) if reverse else k
        row = pl.multiple_of(s * SUBLANES, SUBLANES)
        out = []
        for c in range(LRU_HEADS):
            h, acum = carry[c]
            a = sa_ref[c, pl.ds(row, SUBLANES), :]
            b = sb_ref[c, pl.ds(row, SUBLANES), :]
            h = a * h + b
            acum = a * acum
            hl_ref[c, pl.ds(row, SUBLANES), :] = h
            ac_ref[c, pl.ds(row, SUBLANES), :] = acum
            out.append((h, acum))
        return tuple(out)

    init = tuple((jnp.zeros((SUBLANES, LANES), _F32), jnp.ones((SUBLANES, LANES), _F32))
                 for _ in range(LRU_HEADS))
    return lax.fori_loop(0, SEG, body, init, unroll=2)


def _chain_segments(h_end, a_end, carry_in, reverse):
    order = range(SUBLANES - 1, -1, -1) if reverse else range(SUBLANES)
    states = [None] * SUBLANES
    cur = carry_in
    for j in order:
        states[j] = cur
        cur = h_end[j:j + 1, :] + a_end[j:j + 1, :] * cur
    return states, cur


def _finish_scan(ends, hl_ref, ac_ref, carry_scr, c, reverse):
    states, carry = _chain_segments(ends[c][0], ends[c][1], carry_scr[c:c + 1, :], reverse)
    carry_scr[c:c + 1, :] = carry
    return [_segment_rows(hl_ref, c, j) + _segment_rows(ac_ref, c, j) * states[j]
            for j in range(SUBLANES)]


def _lru_in_kernel(x_ref, xp_ref, xn_ref, g_ref, win_ref, cw_ref, cb_ref, gw_ref, gb_ref,
                   lam_ref, gate_out, hf_out, ab_out, bb_out,
                   rec_scr, xc_scr, sa_scr, sb_scr, hl_scr, ac_scr, carry_scr,
                   *, tiles_per_seq):
    tile = pl.program_id(0)
    t = TOKEN_TILE
    rows = t // CONV_STRIDE
    xa = _halo_tile(x_ref, xp_ref, xn_ref, tiles_per_seq, tile)
    xan = _rms_norm(xa, g_ref[...]).astype(_BF16)
    proj = jnp.dot(xan, win_ref[...], preferred_element_type=_F32)
    gate_out[...] = _gelu_tanh(proj[:t, :D_MODEL])
    for c in range(LRU_HEADS):
        _store_time_ordered(rec_scr, c, proj[:, D_MODEL + c * LANES:D_MODEL + (c + 1) * LANES])

    @pl.when(tile % tiles_per_seq == 0)
    def _():
        carry_scr[...] = jnp.zeros_like(carry_scr)

    half_decay = (-0.5 * LRU_C * LOG2_E) * _softplus(-lam_ref[...])

    for c in range(LRU_HEADS):
        lanes = slice(c * LANES, (c + 1) * LANES)
        cw = cw_ref[:, lanes]
        cb = cb_ref[:, lanes]
        for s in range(CONV_STRIDE):
            acc = cb
            for k in range(LRU_CONV):
                win = rec_scr[c, pl.ds(HALO - 2 + s + k, rows, stride=CONV_STRIDE), :]
                acc = acc + win * cw[k:k + 1, :]
            xc_scr[c % 2, pl.ds(s, rows, stride=CONV_STRIDE), :] = acc
        xc = xc_scr[c % 2]
        th = jnp.tanh(jnp.dot(xc.astype(_BF16), gw_ref[c], preferred_element_type=_F32)
                      + gb_ref[c])
        half_xc = 0.5 * xc
        for d in range(2):
            t_r = th[:, (2 * d) * LANES:(2 * d + 1) * LANES]
            t_i = th[:, (2 * d + 1) * LANES:(2 * d + 2) * LANES]
            hd = half_decay[d:d + 1, lanes]
            a = jnp.exp2(t_r * hd + hd)
            y = 1.0 - a * a
            mult = y * lax.rsqrt(jnp.maximum(y, TINY))
            b = (t_i + 1.0) * (mult * half_xc)
            if d == 0:
                _to_scan_order(sa_scr, c, a)
                _to_scan_order(sb_scr, c, b)
            else:
                ab_out[:, lanes] = a
                bb_out[:, lanes] = b

    ends = _local_scan(sa_scr, sb_scr, hl_scr, ac_scr, reverse=False)
    for c in range(LRU_HEADS):
        segs = _finish_scan(ends, hl_scr, ac_scr, carry_scr, c, reverse=False)
        for j in range(SUBLANES):
            hf_out[j * SEG:(j + 1) * SEG, c * LANES:(c + 1) * LANES] = segs[j]


def _lru_in(x, g, w_in, conv_w, conv_b, gate_w, gate_b, lam, seq_len):
    n = x.shape[0]
    nt = n // TOKEN_TILE
    t = TOKEN_TILE
    tok = pl.BlockSpec((t, D_MODEL), lambda i: (i, 0))
    out = jax.ShapeDtypeStruct((n, D_MODEL), _F32)
    head_scr = pltpu.VMEM((LRU_HEADS, t, LANES), _F32)
    return pl.pallas_call(
        functools.partial(_lru_in_kernel, tiles_per_seq=seq_len // t),
        grid=(nt,),
        in_specs=_halo_specs(n) + [
            _const_spec((1, D_MODEL)),
            _const_spec((D_MODEL, 2 * D_MODEL)),
            _const_spec((LRU_CONV, D_MODEL)),
            _const_spec((1, D_MODEL)),
            _const_spec((LRU_HEADS, LRU_BLOCK, 4 * LRU_BLOCK)),
            _const_spec((LRU_HEADS, 1, 4 * LRU_BLOCK)),
            _const_spec((2, D_MODEL)),
        ],
        out_specs=[tok, tok, tok, tok],
        out_shape=[out, out, out, out],
        scratch_shapes=[
            pltpu.VMEM((LRU_HEADS, t + 2 * HALO, LANES), _F32),
            pltpu.VMEM((2, t, LANES), _F32),
            head_scr, head_scr, head_scr, head_scr,
            pltpu.VMEM((LRU_HEADS, LANES), _F32),
        ],
        compiler_params=pltpu.CompilerParams(
            dimension_semantics=("arbitrary",), vmem_limit_bytes=VMEM_LIMIT),
        name="lru_in",
    )(x, x, x, g, w_in, conv_w, conv_b, gate_w, gate_b, lam)


def _lru_out_kernel(x_ref, gate_ref, hf_ref, ab_ref, bb_ref, wout_ref, out_ref,
                    y_scr, sa_scr, sb_scr, hl_scr, ac_scr, carry_scr,
                    *, tiles_per_seq, n_tiles):
    tile = n_tiles - 1 - pl.program_id(0)

    @pl.when(tile % tiles_per_seq == tiles_per_seq - 1)
    def _():
        carry_scr[...] = jnp.zeros_like(carry_scr)

    for c in range(LRU_HEADS):
        lanes = slice(c * LANES, (c + 1) * LANES)
        _to_scan_order(sa_scr, c, ab_ref[:, lanes])
        _to_scan_order(sb_scr, c, bb_ref[:, lanes])
    ends = _local_scan(sa_scr, sb_scr, hl_scr, ac_scr, reverse=True)
    for c in range(LRU_HEADS):
        lanes = slice(c * LANES, (c + 1) * LANES)
        segs = _finish_scan(ends, hl_scr, ac_scr, carry_scr, c, reverse=True)
        for j in range(SUBLANES):
            rows = slice(j * SEG, (j + 1) * SEG)
            h = hf_ref[rows, lanes] + segs[j]
            y_scr[rows, lanes] = (h * gate_ref[rows, lanes]).astype(_BF16)
    out_ref[...] = x_ref[...] + jnp.dot(y_scr[...], wout_ref[...],
                                        preferred_element_type=_F32)


def _lru_out(x, gate, hf, ab, bb, w_out, seq_len):
    n = x.shape[0]
    nt = n // TOKEN_TILE
    t = TOKEN_TILE
    tok = pl.BlockSpec((t, D_MODEL), lambda i: (nt - 1 - i, 0))
    head_scr = pltpu.VMEM((LRU_HEADS, t, LANES), _F32)
    return pl.pallas_call(
        functools.partial(_lru_out_kernel, tiles_per_seq=seq_len // t, n_tiles=nt),
        grid=(nt,),
        in_specs=[tok, tok, tok, tok, tok, _const_spec((D_MODEL, D_MODEL))],
        out_specs=tok,
        out_shape=jax.ShapeDtypeStruct((n, D_MODEL), _F32),
        scratch_shapes=[
            pltpu.VMEM((t, D_MODEL), _BF16),
            head_scr, head_scr, head_scr, head_scr,
            pltpu.VMEM((LRU_HEADS, LANES), _F32),
        ],
        compiler_params=pltpu.CompilerParams(
            dimension_semantics=("arbitrary",), vmem_limit_bytes=VMEM_LIMIT),
        name="lru_out",
    )(x, gate, hf, ab, bb, w_out)


def _ffn_kernel(x_ref, xp_ref, xn_ref, g_ref, wup_ref, cw_ref, cb_ref, wdn_ref, gfin_ref,
                out_ref, h_scr, acc_scr, *, tiles_per_seq, final_norm):
    tile = pl.program_id(0)
    t = TOKEN_TILE
    rows = t // CONV_STRIDE
    nblk = FF_CHUNK // LANES
    n_chunks = D_FF // FF_CHUNK
    xa = _halo_tile(x_ref, xp_ref, xn_ref, tiles_per_seq, tile)
    xan = _rms_norm(xa, g_ref[...]).astype(_BF16)

    def up_project(ck):
        for half in range(2):
            h = jnp.dot(xan, wup_ref[half * n_chunks + ck], preferred_element_type=_F32)
            for b in range(nblk):
                _store_time_ordered(h_scr, (ck % 2) * 2 * nblk + half * nblk + b,
                                    h[:, b * LANES:(b + 1) * LANES])

    def gated_conv(ck):
        blocks = []
        for b in range(nblk):
            lanes = slice(b * LANES, (b + 1) * LANES)
            pieces = []
            for s in range(CONV_STRIDE):
                conv = []
                for half in range(2):
                    cw = cw_ref[half * n_chunks + ck, :, lanes]
                    val = cb_ref[half * n_chunks + ck, :, lanes]
                    for k in range(3):
                        win = h_scr[(ck % 2) * 2 * nblk + half * nblk + b,
                                    pl.ds(HALO - 1 + s + k, rows, stride=CONV_STRIDE), :]
                        val = val + win * cw[k:k + 1, :]
                    conv.append(val)
                pieces.append(_gelu_tanh(conv[0]) * conv[1])
            blocks.append(jnp.concatenate(pieces, axis=0))
        return jnp.concatenate(blocks, axis=1).astype(_BF16)

    acc = jnp.zeros((t, D_MODEL), _F32)
    up_project(0)
    for ck in range(n_chunks):
        if ck + 1 < n_chunks:
            up_project(ck + 1)
        acc = acc + jnp.dot(gated_conv(ck), wdn_ref[ck], preferred_element_type=_F32)
    for c in range(D_MODEL // LANES):
        for s in range(CONV_STRIDE):
            acc_scr[c, pl.ds(s, rows, stride=CONV_STRIDE), :] = (
                acc[s * rows:(s + 1) * rows, c * LANES:(c + 1) * LANES])
    y = x_ref[...] + jnp.concatenate(
        [acc_scr[c] for c in range(D_MODEL // LANES)], axis=1)
    if final_norm:
        y = _rms_norm(y, gfin_ref[...])
    out_ref[...] = y


def _conv_ffn(x, g, w_up, conv_w, conv_b, w_down, g_final, seq_len, final_norm):
    n = x.shape[0]
    t = TOKEN_TILE
    nch = D_FF // FF_CHUNK
    return pl.pallas_call(
        functools.partial(_ffn_kernel, tiles_per_seq=seq_len // t, final_norm=final_norm),
        grid=(n // t,),
        in_specs=_halo_specs(n) + [
            _const_spec((1, D_MODEL)),
            _const_spec((2 * nch, D_MODEL, FF_CHUNK)),
            _const_spec((2 * nch, 3, FF_CHUNK)),
            _const_spec((2 * nch, 1, FF_CHUNK)),
            _const_spec((nch, FF_CHUNK, D_MODEL)),
            _const_spec((1, D_MODEL)),
        ],
        out_specs=pl.BlockSpec((t, D_MODEL), lambda i: (i, 0)),
        out_shape=jax.ShapeDtypeStruct((n, D_MODEL), _F32),
        scratch_shapes=[
            pltpu.VMEM((4 * FF_CHUNK // LANES, t + 2 * HALO, LANES), _F32),
            pltpu.VMEM((D_MODEL // LANES, t, LANES), _F32),
        ],
        compiler_params=pltpu.CompilerParams(
            dimension_semantics=("arbitrary",), vmem_limit_bytes=VMEM_LIMIT),
        name="conv_ffn",
    )(x, x, x, g, w_up, conv_w, conv_b, w_down, g_final)


def _chunk_major(w):
    nch = D_FF // FF_CHUNK
    w = w.reshape(w.shape[:-1] + (2 * nch, FF_CHUNK))
    return jnp.moveaxis(w, -2, 0)


def _qkv_kernel(x_ref, g_ref, w_ref, q_ref, k_ref, v_ref):
    xn = _rms_norm(x_ref[...], g_ref[...]).astype(_BF16)
    qkv = jnp.dot(xn, w_ref[...], preferred_element_type=_F32)
    q_ref[...] = (qkv[:, :D_MODEL] * (HEAD_DIM ** -0.5 * LOG2_E)).astype(_BF16)
    k_ref[...] = qkv[:, D_MODEL:2 * D_MODEL].astype(_BF16)
    v_ref[...] = qkv[:, 2 * D_MODEL:].astype(_BF16)


def _qkv(x, g, w_qkv):
    n = x.shape[0]
    t = TOKEN_TILE
    tok = pl.BlockSpec((t, D_MODEL), lambda i: (i, 0))
    out = jax.ShapeDtypeStruct((n, D_MODEL), _BF16)
    return pl.pallas_call(
        _qkv_kernel,
        grid=(n // t,),
        in_specs=[tok, _const_spec((1, D_MODEL)), _const_spec((D_MODEL, 3 * D_MODEL))],
        out_specs=[tok, tok, tok],
        out_shape=[out, out, out],
        compiler_params=pltpu.CompilerParams(
            dimension_semantics=("arbitrary",), vmem_limit_bytes=VMEM_LIMIT),
        name="qkv",
    )(x, g, w_qkv)


def _attn_kernel(q_ref, k_ref, v_ref, bias_ref, x_ref, wo_ref, out_ref, s_scr, o_scr,
                 *, blocks_per_seq, n_tiles):
    win = WIN_ROWS * GRID_W
    n_pairs = ATTN_HEADS // 2
    rows_per_seq = blocks_per_seq * ROW_BLOCK
    tile = pl.program_id(0)
    blk = tile % blocks_per_seq
    win_row0 = (jnp.clip(tile - 1, 0, n_tiles - 3) - (tile - blk)) * ROW_BLOCK
    lane = lax.broadcasted_iota(jnp.int32, (GRID_W, LANES), 1)
    low = lane < HEAD_DIM
    ones = jnp.ones((win, LANES), _BF16)

    def row_body(qr, carry):
        r = blk * ROW_BLOCK + qr
        rs = jnp.clip(r - WIN_ROWS // 2, 0, rows_per_seq - WIN_ROWS)
        shift = (WIN_ROWS - 1) - (r - rs)
        par = shift % 2
        m0 = shift // 2
        kstart = pl.multiple_of((rs - win_row0) * GRID_W, GRID_W)
        qrow = pl.multiple_of(qr * GRID_W, GRID_W)
        row_max = []
        for hp in range(n_pairs):
            lanes = slice(hp * LANES, (hp + 1) * LANES)
            q2 = q_ref[pl.ds(qrow, GRID_W), lanes]
            zero = jnp.zeros_like(q2)
            qs = jnp.concatenate([jnp.where(low, q2, zero), jnp.where(low, zero, q2)], axis=0)
            s = lax.dot_general(qs, k_ref[pl.ds(kstart, win), lanes], (((1,), (1,)), ((), ())),
                                preferred_element_type=_F32)
            bias = jnp.concatenate(
                [jnp.concatenate([bias_ref[par, 2 * hp + hh, m0 + m] for m in range(4)], axis=1)
                 for hh in range(2)], axis=0)
            s = s + bias
            s_scr[hp] = s
            row_max.append(jnp.max(s, axis=-1, keepdims=True))
        for hp in range(n_pairs):
            lanes = slice(hp * LANES, (hp + 1) * LANES)
            p = jnp.exp2(s_scr[hp] - row_max[hp]).astype(_BF16)
            v_ones = jnp.concatenate([v_ref[pl.ds(kstart, win), lanes], ones], axis=1)
            ov = jnp.dot(p, v_ones, preferred_element_type=_F32)
            o = ov[:, :LANES] * (1.0 / ov[:, LANES:])
            o2 = jnp.where(low, o[:GRID_W], o[GRID_W:])
            o_scr[pl.ds(qrow, GRID_W), lanes] = o2.astype(_BF16)
        return carry

    lax.fori_loop(0, ROW_BLOCK, row_body, 0)
    out_ref[...] = x_ref[...] + jnp.dot(o_scr[...], wo_ref[...], preferred_element_type=_F32)


def _attention(x, q, k, v, bias_tab, w_o, seq_len):
    n = x.shape[0]
    t = TOKEN_TILE
    nt = n // t
    tok = pl.BlockSpec((t, D_MODEL), lambda i: (i, 0))
    window = pl.BlockSpec((pl.Element(3 * t), pl.Element(D_MODEL)),
                          lambda i: (jnp.clip(i - 1, 0, nt - 3) * t, 0))
    return pl.pallas_call(
        functools.partial(_attn_kernel, blocks_per_seq=seq_len // t, n_tiles=nt),
        grid=(nt,),
        in_specs=[tok, window, window, _const_spec(bias_tab.shape), tok,
                  _const_spec((D_MODEL, D_MODEL))],
        out_specs=tok,
        out_shape=jax.ShapeDtypeStruct((n, D_MODEL), _F32),
        scratch_shapes=[
            pltpu.VMEM((ATTN_HEADS // 2, 2 * GRID_W, WIN_ROWS * GRID_W), _F32),
            pltpu.VMEM((t, D_MODEL), _BF16),
        ],
        compiler_params=pltpu.CompilerParams(
            dimension_semantics=("arbitrary",), vmem_limit_bytes=VMEM_LIMIT),
        name="attention",
    )(q, k, v, bias_tab, x, w_o)


def _bias_table(rpb):
    col = jnp.arange(GRID_W, dtype=jnp.int32)
    col_start = jnp.clip(col - WIN_COLS // 2, 0, GRID_W - WIN_COLS)
    kc = jnp.arange(GRID_W, dtype=jnp.int32)
    valid = (kc[None, :] >= col_start[:, None]) & (kc[None, :] < col_start[:, None] + WIN_COLS)
    dc = jnp.clip(kc[None, :] - col[:, None] + (WIN_COLS - 1), 0, 2 * WIN_COLS - 2)
    full = (rpb.astype(_F32) * LOG2_E)[:, :, dc]
    full = jnp.where(valid[None, None], full, NEG_BIG)
    flat = jnp.transpose(full, (0, 2, 1, 3)).reshape(ATTN_HEADS, GRID_W, -1)
    flat = jnp.pad(flat, ((0, 0), (0, 0), (0, 64)), constant_values=NEG_BIG)
    tabs = []
    for par in range(2):
        sl = flat[:, :, 64 * par:64 * par + 7 * LANES]
        tabs.append(jnp.transpose(sl.reshape(ATTN_HEADS, GRID_W, 7, LANES), (0, 2, 1, 3)))
    return jnp.stack(tabs, axis=0)


def _trunk(x, seq_len, p):
    gate, hf, ab, bb = _lru_in(x, p["norm_mix"][0], p["w_in"], p["lru_conv_w"], p["lru_conv_b"],
                               p["gate_w"], p["gate_b"], p["lam"], seq_len)
    x = _lru_out(x, gate, hf, ab, bb, p["w_out"], seq_len)
    x = _conv_ffn(x, p["norm_ffn"][0], p["w_up"][0], p["ffn_conv_w"][0], p["ffn_conv_b"][0],
                  p["w_down"][0], p["norm_final"], seq_len, final_norm=False)
    q, k, v = _qkv(x, p["norm_mix"][1], p["w_qkv"])
    x = _attention(x, q, k, v, p["bias_tab"], p["w_o"], seq_len)
    x = _conv_ffn(x, p["norm_ffn"][1], p["w_up"][1], p["ffn_conv_w"][1], p["ffn_conv_b"][1],
                  p["w_down"][1], p["norm_final"], seq_len, final_norm=True)
    return x


def kernel(x_prompt, x_sample, norm_mix, norm_ffn, norm_final, lru_w_in, lru_conv_w, lru_conv_b, lru_gate_w, lru_gate_b, lru_lambda, lru_w_out, attn_w_qkv, attn_rpb, attn_w_o, ffn_w_up, ffn_conv_w, ffn_conv_b, ffn_w_down):
    gw = (0.5 * jnp.transpose(lru_gate_w[0], (2, 3, 0, 1, 4))).reshape(
        LRU_HEADS, LRU_BLOCK, 4 * LRU_BLOCK).astype(_BF16)
    gb = (0.5 * jnp.transpose(lru_gate_b[0], (2, 0, 1, 3))).reshape(
        LRU_HEADS, 1, 4 * LRU_BLOCK)
    p = {
        "norm_mix": [norm_mix[i][None, :] for i in range(2)],
        "norm_ffn": [norm_ffn[i][None, :] for i in range(2)],
        "norm_final": norm_final[None, :],
        "w_in": lru_w_in[0].astype(_BF16),
        "lru_conv_w": lru_conv_w[0],
        "lru_conv_b": lru_conv_b[0][None, :],
        "gate_w": gw,
        "gate_b": gb,
        "lam": lru_lambda[0],
        "w_out": lru_w_out[0].astype(_BF16),
        "w_qkv": attn_w_qkv[0].astype(_BF16),
        "bias_tab": _bias_table(attn_rpb[0]),
        "w_o": attn_w_o[0].astype(_BF16),
        "w_up": [_chunk_major(ffn_w_up[i].astype(_BF16)) for i in range(2)],
        "ffn_conv_w": [_chunk_major(ffn_conv_w[i]) for i in range(2)],
        "ffn_conv_b": [_chunk_major(ffn_conv_b[i][None, :]) for i in range(2)],
        "w_down": [ffn_w_down[i].astype(_BF16).reshape(D_FF // FF_CHUNK, FF_CHUNK, D_MODEL)
                   for i in range(2)],
    }
    outs = []
    for x in (x_prompt, x_sample):
        b, seq_len, d = x.shape
        y = _trunk(x.reshape(b * seq_len, d), seq_len, p)
        outs.append(y.reshape(b, seq_len, d))
    return tuple(outs)
```

```python
import functools

import jax
import jax.numpy as jnp
from jax import lax
from jax.experimental import pallas as pl
from jax.experimental.pallas import tpu as pltpu

D_MODEL = 1024
LRU_HEADS = 8
LRU_BLOCK = 128
LRU_CONV = 4
LRU_C = 8.0
ATTN_HEADS = 16
HEAD_DIM = 64
GRID_W = 64
WIN_ROWS = 8
WIN_COLS = 16
D_FF = 2816
NORM_EPS = 1e-6

LANES = 128
SUBLANES = 8
TOKEN_TILE = 512
HALO = SUBLANES
SEG = TOKEN_TILE // SUBLANES
FF_CHUNK = 256
CONV_STRIDE = 4
ROW_BLOCK = TOKEN_TILE // GRID_W
VMEM_LIMIT = 56 * 1024 * 1024
NEG_BIG = -1e30
TINY = 1e-30
LOG2_E = 1.4426950408889634

_BF16 = jnp.bfloat16
_F32 = jnp.float32


def _const_spec(shape):
    nd = len(shape)
    return pl.BlockSpec(shape, lambda i: (0,) * nd, pipeline_mode=pl.Buffered(1))


def _rms_norm(x, g):
    ms = jnp.mean(x * x, axis=-1, keepdims=True)
    return x * lax.rsqrt(ms + NORM_EPS) * g


def _gelu_tanh(x):
    k0 = -2.0 * LOG2_E * 0.7978845608028654
    return x / (1.0 + jnp.exp2(x * ((k0 * 0.044715) * (x * x) + k0)))


def _softplus(z):
    e = jnp.exp(-jnp.abs(z))
    u = 1.0 + e
    log1p = jnp.where(u == 1.0, e, jnp.log(u) * (e / (u - 1.0)))
    return jnp.maximum(z, 0.0) + log1p


def _halo_tile(x_ref, xp_ref, xn_ref, tiles_per_seq, tile):
    t_in_seq = tile % tiles_per_seq
    keep_prev = jnp.where(t_in_seq == 0, 0.0, 1.0)
    keep_next = jnp.where(t_in_seq == tiles_per_seq - 1, 0.0, 1.0)
    return jnp.concatenate(
        [x_ref[...], xp_ref[...] * keep_prev, xn_ref[...] * keep_next], axis=0)


def _store_time_ordered(dst_ref, slot, val):
    t = TOKEN_TILE
    dst_ref[slot, HALO:HALO + t, :] = val[:t]
    dst_ref[slot, 0:HALO, :] = val[t:t + HALO]
    dst_ref[slot, HALO + t:, :] = val[t + HALO:]


def _halo_specs(n_tokens):
    per = TOKEN_TILE // HALO
    last = n_tokens // HALO - 1
    return [
        pl.BlockSpec((TOKEN_TILE, D_MODEL), lambda i: (i, 0)),
        pl.BlockSpec((HALO, D_MODEL), lambda i: (jnp.maximum(i * per - 1, 0), 0)),
        pl.BlockSpec((HALO, D_MODEL), lambda i: (jnp.minimum((i + 1) * per, last), 0)),
    ]


def _to_scan_order(dst_ref, c, val):
    for j in range(SUBLANES):
        dst_ref[c, pl.ds(j, SEG, stride=SUBLANES), :] = val[j * SEG:(j + 1) * SEG, :]


def _segment_rows(src_ref, c, j):
    return src_ref[c, pl.ds(j, SEG, stride=SUBLANES), :]


def _local_scan(sa_ref, sb_ref, hl_ref, ac_ref, reverse):
    def body(k, carry):
        s = (SEG - 1 - k) if reverse else k
        row = pl.multiple_of(s * SUBLANES, SUBLANES)
        out = []
        for c in range(LRU_HEADS):
            h, acum = carry[c]
            a = sa_ref[c, pl.ds(row, SUBLANES), :]
            b = sb_ref[c, pl.ds(row, SUBLANES), :]
            h = a * h + b
            acum = a * acum
            hl_ref[c, pl.ds(row, SUBLANES), :] = h
            ac_ref[c, pl.ds(row, SUBLANES), :] = acum
            out.append((h, acum))
        return tuple(out)

    init = tuple((jnp.zeros((SUBLANES, LANES), _F32), jnp.ones((SUBLANES, LANES), _F32))
                 for _ in range(LRU_HEADS))
    return lax.fori_loop(0, SEG, body, init, unroll=2)


def _chain_segments(h_end, a_end, carry_in, reverse):
    order = range(SUBLANES - 1, -1, -1) if reverse else range(SUBLANES)
    states = [None] * SUBLANES
    cur = carry_in
    for j in order:
        states[j] = cur
        cur = h_end[j:j + 1, :] + a_end[j:j + 1, :] * cur
    return states, cur


def _finish_scan(ends, hl_ref, ac_ref, carry_scr, c, reverse):
    states, carry = _chain_segments(ends[c][0], ends[c][1], carry_scr[c:c + 1, :], reverse)
    carry_scr[c:c + 1, :] = carry
    return [_segment_rows(hl_ref, c, j) + _segment_rows(ac_ref, c, j) * states[j]
            for j in range(SUBLANES)]


def _lru_in_kernel(x_ref, xp_ref, xn_ref, g_ref, win_ref, cw_ref, cb_ref, gw_ref,
                   lam_ref, gate_out, hf_out, ab_out, bb_out,
                   rec_scr, xc_scr, sa_scr, sb_scr, hl_scr, ac_scr, carry_scr,
                   *, tiles_per_seq):
    tile = pl.program_id(0)
    t = TOKEN_TILE
    rows = t // CONV_STRIDE
    xa = _halo_tile(x_ref, xp_ref, xn_ref, tiles_per_seq, tile)
    xan = _rms_norm(xa, g_ref[...]).astype(_BF16)
    proj = jnp.dot(xan, win_ref[...], preferred_element_type=_F32)
    gate_out[...] = _gelu_tanh(proj[:t, :D_MODEL]).astype(_BF16)
    for c in range(LRU_HEADS):
        _store_time_ordered(rec_scr, c, proj[:, D_MODEL + c * LANES:D_MODEL + (c + 1) * LANES])

    @pl.when(tile % tiles_per_seq == 0)
    def _():
        carry_scr[...] = jnp.zeros_like(carry_scr)

    half_decay = (-0.5 * LRU_C * LOG2_E) * _softplus(-lam_ref[...])

    for c in range(LRU_HEADS):
        lanes = slice(c * LANES, (c + 1) * LANES)
        cw = cw_ref[:, lanes]
        cb = cb_ref[:, lanes]
        for s in range(CONV_STRIDE):
            acc = cb
            for k in range(LRU_CONV):
                win = rec_scr[c, pl.ds(HALO - 2 + s + k, rows, stride=CONV_STRIDE), :]
                acc = acc + win * cw[k:k + 1, :]
            xc_scr[c % 2, pl.ds(s, rows, stride=CONV_STRIDE), :] = acc
        half_xc = xc_scr[c % 2]
        lhs = jnp.concatenate([half_xc.astype(_BF16), jnp.ones((t, LANES), _BF16)], axis=1)
        th = jnp.tanh(jnp.dot(lhs, gw_ref[c], preferred_element_type=_F32))
        for d in range(2):
            t_r = th[:, (2 * d) * LANES:(2 * d + 1) * LANES]
            t_i = th[:, (2 * d + 1) * LANES:(2 * d + 2) * LANES]
            hd = half_decay[d:d + 1, lanes]
            a = jnp.exp2(t_r * hd + hd)
            y = 1.0 - a * a
            mult = y * lax.rsqrt(jnp.maximum(y, TINY))
            b = (t_i + 1.0) * (mult * half_xc)
            if d == 0:
                _to_scan_order(sa_scr, c, a)
                _to_scan_order(sb_scr, c, b)
            else:
                ab_out[:, lanes] = a
                bb_out[:, lanes] = b

    ends = _local_scan(sa_scr, sb_scr, hl_scr, ac_scr, reverse=False)
    for c in range(LRU_HEADS):
        segs = _finish_scan(ends, hl_scr, ac_scr, carry_scr, c, reverse=False)
        for j in range(SUBLANES):
            hf_out[j * SEG:(j + 1) * SEG, c * LANES:(c + 1) * LANES] = segs[j].astype(_BF16)


def _lru_in(x, g, w_in, conv_w, conv_b, gate_w, lam, seq_len):
    n = x.shape[0]
    nt = n // TOKEN_TILE
    t = TOKEN_TILE
    tok = pl.BlockSpec((t, D_MODEL), lambda i: (i, 0))
    out = jax.ShapeDtypeStruct((n, D_MODEL), _F32)
    out16 = jax.ShapeDtypeStruct((n, D_MODEL), _BF16)
    head_scr = pltpu.VMEM((LRU_HEADS, t, LANES), _F32)
    return pl.pallas_call(
        functools.partial(_lru_in_kernel, tiles_per_seq=seq_len // t),
        grid=(nt,),
        in_specs=_halo_specs(n) + [
            _const_spec((1, D_MODEL)),
            _const_spec((D_MODEL, 2 * D_MODEL)),
            _const_spec((LRU_CONV, D_MODEL)),
            _const_spec((1, D_MODEL)),
            _const_spec((LRU_HEADS, 2 * LRU_BLOCK, 4 * LRU_BLOCK)),
            _const_spec((2, D_MODEL)),
        ],
        out_specs=[tok, tok, tok, tok],
        out_shape=[out16, out16, out, out],
        scratch_shapes=[
            pltpu.VMEM((LRU_HEADS, t + 2 * HALO, LANES), _F32),
            pltpu.VMEM((2, t, LANES), _F32),
            head_scr, head_scr, head_scr, head_scr,
            pltpu.VMEM((LRU_HEADS, LANES), _F32),
        ],
        compiler_params=pltpu.CompilerParams(
            dimension_semantics=("arbitrary",), vmem_limit_bytes=VMEM_LIMIT),
        name="lru_in",
    )(x, x, x, g, w_in, conv_w, conv_b, gate_w, lam)


def _lru_out_kernel(x_ref, gate_ref, hf_ref, ab_ref, bb_ref, wout_ref, out_ref,
                    y_scr, sa_scr, sb_scr, hl_scr, ac_scr, carry_scr,
                    *, tiles_per_seq, n_tiles):
    tile = n_tiles - 1 - pl.program_id(0)

    @pl.when(tile % tiles_per_seq == tiles_per_seq - 1)
    def _():
        carry_scr[...] = jnp.zeros_like(carry_scr)

    for c in range(LRU_HEADS):
        lanes = slice(c * LANES, (c + 1) * LANES)
        _to_scan_order(sa_scr, c, ab_ref[:, lanes])
        _to_scan_order(sb_scr, c, bb_ref[:, lanes])
    ends = _local_scan(sa_scr, sb_scr, hl_scr, ac_scr, reverse=True)
    for c in range(LRU_HEADS):
        lanes = slice(c * LANES, (c + 1) * LANES)
        segs = _finish_scan(ends, hl_scr, ac_scr, carry_scr, c, reverse=True)
        for j in range(SUBLANES):
            rows = slice(j * SEG, (j + 1) * SEG)
            h = hf_ref[rows, lanes].astype(_F32) + segs[j]
            y_scr[rows, lanes] = (h * gate_ref[rows, lanes].astype(_F32)).astype(_BF16)
    out_ref[...] = x_ref[...] + jnp.dot(y_scr[...], wout_ref[...],
                                        preferred_element_type=_F32)


def _lru_out(x, gate, hf, ab, bb, w_out, seq_len):
    n = x.shape[0]
    nt = n // TOKEN_TILE
    t = TOKEN_TILE
    tok = pl.BlockSpec((t, D_MODEL), lambda i: (nt - 1 - i, 0))
    head_scr = pltpu.VMEM((LRU_HEADS, t, LANES), _F32)
    return pl.pallas_call(
        functools.partial(_lru_out_kernel, tiles_per_seq=seq_len // t, n_tiles=nt),
        grid=(nt,),
        in_specs=[tok, tok, tok, tok, tok, _const_spec((D_MODEL, D_MODEL))],
        out_specs=tok,
        out_shape=jax.ShapeDtypeStruct((n, D_MODEL), _F32),
        scratch_shapes=[
            pltpu.VMEM((t, D_MODEL), _BF16),
            head_scr, head_scr, head_scr, head_scr,
            pltpu.VMEM((LRU_HEADS, LANES), _F32),
        ],
        compiler_params=pltpu.CompilerParams(
            dimension_semantics=("arbitrary",), vmem_limit_bytes=VMEM_LIMIT),
        name="lru_out",
    )(x, gate, hf, ab, bb, w_out)


def _ffn_kernel(x_ref, xp_ref, xn_ref, g_ref, wup_ref, cw_ref, cb_ref, wdn_ref, gfin_ref,
                out_ref, h_scr, acc_scr, *, tiles_per_seq, final_norm):
    tile = pl.program_id(0)
    t = TOKEN_TILE
    rows = t // CONV_STRIDE
    nblk = FF_CHUNK // LANES
    n_chunks = D_FF // FF_CHUNK
    xa = _halo_tile(x_ref, xp_ref, xn_ref, tiles_per_seq, tile)
    xan = _rms_norm(xa, g_ref[...]).astype(_BF16)

    def up_project(ck):
        for half in range(2):
            col0 = half * D_FF + ck * FF_CHUNK
            h = jnp.dot(xan, wup_ref[:, col0:col0 + FF_CHUNK], preferred_element_type=_F32)
            for b in range(nblk):
                _store_time_ordered(h_scr, (ck % 2) * 2 * nblk + half * nblk + b,
                                    h[:, b * LANES:(b + 1) * LANES])

    def gated_conv(ck):
        blocks = []
        for b in range(nblk):
            pieces = []
            for s in range(CONV_STRIDE):
                conv = []
                for half in range(2):
                    col = half * D_FF + ck * FF_CHUNK + b * LANES
                    cw = cw_ref[:, col:col + LANES]
                    val = cb_ref[:, col:col + LANES]
                    for k in range(3):
                        win = h_scr[(ck % 2) * 2 * nblk + half * nblk + b,
                                    pl.ds(HALO - 1 + s + k, rows, stride=CONV_STRIDE), :]
                        val = val + win * cw[k:k + 1, :]
                    conv.append(val)
                pieces.append(_gelu_tanh(conv[0]) * conv[1])
            blocks.append(jnp.concatenate(pieces, axis=0))
        return jnp.concatenate(blocks, axis=1).astype(_BF16)

    acc = jnp.zeros((t, D_MODEL), _F32)
    up_project(0)
    for ck in range(n_chunks):
        if ck + 1 < n_chunks:
            up_project(ck + 1)
        acc = acc + jnp.dot(gated_conv(ck), wdn_ref[ck * FF_CHUNK:(ck + 1) * FF_CHUNK, :],
                            preferred_element_type=_F32)
    for c in range(D_MODEL // LANES):
        for s in range(CONV_STRIDE):
            acc_scr[c, pl.ds(s, rows, stride=CONV_STRIDE), :] = (
                acc[s * rows:(s + 1) * rows, c * LANES:(c + 1) * LANES])
    y = x_ref[...] + jnp.concatenate(
        [acc_scr[c] for c in range(D_MODEL // LANES)], axis=1)
    if final_norm:
        y = _rms_norm(y, gfin_ref[...])
    out_ref[...] = y


def _conv_ffn(x, g, w_up, conv_w, conv_b, w_down, g_final, seq_len, final_norm):
    n = x.shape[0]
    t = TOKEN_TILE
    return pl.pallas_call(
        functools.partial(_ffn_kernel, tiles_per_seq=seq_len // t, final_norm=final_norm),
        grid=(n // t,),
        in_specs=_halo_specs(n) + [
            _const_spec((1, D_MODEL)),
            _const_spec((D_MODEL, 2 * D_FF)),
            _const_spec((3, 2 * D_FF)),
            _const_spec((1, 2 * D_FF)),
            _const_spec((D_FF, D_MODEL)),
            _const_spec((1, D_MODEL)),
        ],
        out_specs=pl.BlockSpec((t, D_MODEL), lambda i: (i, 0)),
        out_shape=jax.ShapeDtypeStruct((n, D_MODEL), _F32),
        scratch_shapes=[
            pltpu.VMEM((4 * FF_CHUNK // LANES, t + 2 * HALO, LANES), _F32),
            pltpu.VMEM((D_MODEL // LANES, t, LANES), _F32),
        ],
        compiler_params=pltpu.CompilerParams(
            dimension_semantics=("arbitrary",), vmem_limit_bytes=VMEM_LIMIT),
        name="conv_ffn",
    )(x, x, x, g, w_up, conv_w, conv_b, w_down, g_final)


def _qkv_kernel(x_ref, g_ref, w_ref, q_ref, k_ref, v_ref):
    xn = _rms_norm(x_ref[...], g_ref[...]).astype(_BF16)
    qkv = jnp.dot(xn, w_ref[...], preferred_element_type=_F32)
    q_ref[...] = (qkv[:, :D_MODEL] * (HEAD_DIM ** -0.5 * LOG2_E)).astype(_BF16)
    k_ref[...] = qkv[:, D_MODEL:2 * D_MODEL].astype(_BF16)
    v_ref[...] = qkv[:, 2 * D_MODEL:].astype(_BF16)


def _qkv(x, g, w_qkv):
    n = x.shape[0]
    t = TOKEN_TILE
    tok = pl.BlockSpec((t, D_MODEL), lambda i: (i, 0))
    out = jax.ShapeDtypeStruct((n, D_MODEL), _BF16)
    return pl.pallas_call(
        _qkv_kernel,
        grid=(n // t,),
        in_specs=[tok, _const_spec((1, D_MODEL)), _const_spec((D_MODEL, 3 * D_MODEL))],
        out_specs=[tok, tok, tok],
        out_shape=[out, out, out],
        compiler_params=pltpu.CompilerParams(
            dimension_semantics=("arbitrary",), vmem_limit_bytes=VMEM_LIMIT),
        name="qkv",
    )(x, g, w_qkv)


def _attn_kernel(q_ref, k_ref, v_ref, bias_ref, x_ref, wo_ref, out_ref, s_scr, o_scr,
                 *, blocks_per_seq, n_tiles):
    win = WIN_ROWS * GRID_W
    n_pairs = ATTN_HEADS // 2
    rows_per_seq = blocks_per_seq * ROW_BLOCK
    tile = pl.program_id(0)
    blk = tile % blocks_per_seq
    win_row0 = (jnp.clip(tile - 1, 0, n_tiles - 3) - (tile - blk)) * ROW_BLOCK
    lane = lax.broadcasted_iota(jnp.int32, (GRID_W, LANES), 1)
    low = lane < HEAD_DIM
    ones = jnp.ones((win, LANES), _BF16)

    def row_body(qr, carry):
        r = blk * ROW_BLOCK + qr
        rs = jnp.clip(r - WIN_ROWS // 2, 0, rows_per_seq - WIN_ROWS)
        shift = (WIN_ROWS - 1) - (r - rs)
        par = shift % 2
        m0 = shift // 2
        kstart = pl.multiple_of((rs - win_row0) * GRID_W, GRID_W)
        qrow = qr * GRID_W
        row_max = []
        for hp in range(n_pairs):
            lanes = slice(hp * LANES, (hp + 1) * LANES)
            q2 = q_ref[pl.ds(qrow, GRID_W), lanes]
            zero = jnp.zeros_like(q2)
            qs = jnp.concatenate([jnp.where(low, q2, zero), jnp.where(low, zero, q2)], axis=0)
            s = lax.dot_general(qs, k_ref[pl.ds(kstart, win), lanes], (((1,), (1,)), ((), ())),
                                preferred_element_type=_F32)
            bias = jnp.concatenate(
                [jnp.concatenate([bias_ref[par, 2 * hp + hh, m0 + m] for m in range(4)], axis=1)
                 for hh in range(2)], axis=0)
            s = s + bias
            s_scr[(qr % 2) * n_pairs + hp] = s
            row_max.append(jnp.max(s, axis=-1, keepdims=True))
        for hp in range(n_pairs):
            lanes = slice(hp * LANES, (hp + 1) * LANES)
            p = jnp.exp2(s_scr[(qr % 2) * n_pairs + hp] - row_max[hp]).astype(_BF16)
            v_ones = jnp.concatenate([v_ref[pl.ds(kstart, win), lanes], ones], axis=1)
            ov = jnp.dot(p, v_ones, preferred_element_type=_F32)
            o = ov[:, :LANES] * (1.0 / ov[:, LANES:])
            o2 = jnp.where(low, o[:GRID_W], o[GRID_W:])
            o_scr[pl.ds(qrow, GRID_W), lanes] = o2.astype(_BF16)
        return carry

    for qr in range(ROW_BLOCK):
        row_body(qr, 0)
    out_ref[...] = x_ref[...] + jnp.dot(o_scr[...], wo_ref[...], preferred_element_type=_F32)


def _attention(x, q, k, v, bias_tab, w_o, seq_len):
    n = x.shape[0]
    t = TOKEN_TILE
    nt = n // t
    tok = pl.BlockSpec((t, D_MODEL), lambda i: (i, 0))
    window = pl.BlockSpec((pl.Element(3 * t), pl.Element(D_MODEL)),
                          lambda i: (jnp.clip(i - 1, 0, nt - 3) * t, 0))
    return pl.pallas_call(
        functools.partial(_attn_kernel, blocks_per_seq=seq_len // t, n_tiles=nt),
        grid=(nt,),
        in_specs=[tok, window, window, _const_spec(bias_tab.shape), tok,
                  _const_spec((D_MODEL, D_MODEL))],
        out_specs=tok,
        out_shape=jax.ShapeDtypeStruct((n, D_MODEL), _F32),
        scratch_shapes=[
            pltpu.VMEM((ATTN_HEADS, 2 * GRID_W, WIN_ROWS * GRID_W), _F32),
            pltpu.VMEM((t, D_MODEL), _BF16),
        ],
        compiler_params=pltpu.CompilerParams(
            dimension_semantics=("arbitrary",), vmem_limit_bytes=VMEM_LIMIT),
        name="attention",
    )(q, k, v, bias_tab, x, w_o)


def _bias_table(rpb):
    col = jnp.arange(GRID_W, dtype=jnp.int32)
    col_start = jnp.clip(col - WIN_COLS // 2, 0, GRID_W - WIN_COLS)
    kc = jnp.arange(GRID_W, dtype=jnp.int32)
    valid = (kc[None, :] >= col_start[:, None]) & (kc[None, :] < col_start[:, None] + WIN_COLS)
    pad = GRID_W - WIN_COLS
    padded = jnp.pad(rpb.astype(_F32) * LOG2_E, ((0, 0), (0, 0), (pad, pad)))
    full = jnp.stack([padded[:, :, GRID_W - 1 - c:2 * GRID_W - 1 - c] for c in range(GRID_W)],
                     axis=2)
    full = jnp.where(valid[None, None], full, NEG_BIG)
    flat = jnp.transpose(full, (0, 2, 1, 3)).reshape(ATTN_HEADS, GRID_W, -1)
    flat = jnp.pad(flat, ((0, 0), (0, 0), (0, 64)), constant_values=NEG_BIG)
    tabs = []
    for par in range(2):
        sl = flat[:, :, 64 * par:64 * par + 7 * LANES]
        tabs.append(jnp.transpose(sl.reshape(ATTN_HEADS, GRID_W, 7, LANES), (0, 2, 1, 3)))
    return jnp.stack(tabs, axis=0)


def _trunk(x, seq_len, p):
    gate, hf, ab, bb = _lru_in(x, p["norm_mix"][0], p["w_in"], p["lru_conv_w"], p["lru_conv_b"],
                               p["gate_w"], p["lam"], seq_len)
    x = _lru_out(x, gate, hf, ab, bb, p["w_out"], seq_len)
    x = _conv_ffn(x, p["norm_ffn"][0], p["w_up"][0], p["ffn_conv_w"][0], p["ffn_conv_b"][0],
                  p["w_down"][0], p["norm_final"], seq_len, final_norm=False)
    q, k, v = _qkv(x, p["norm_mix"][1], p["w_qkv"])
    x = _attention(x, q, k, v, p["bias_tab"], p["w_o"], seq_len)
    x = _conv_ffn(x, p["norm_ffn"][1], p["w_up"][1], p["ffn_conv_w"][1], p["ffn_conv_b"][1],
                  p["w_down"][1], p["norm_final"], seq_len, final_norm=True)
    return x


def kernel(x_prompt, x_sample, norm_mix, norm_ffn, norm_final, lru_w_in, lru_conv_w, lru_conv_b, lru_gate_w, lru_gate_b, lru_lambda, lru_w_out, attn_w_qkv, attn_rpb, attn_w_o, ffn_w_up, ffn_conv_w, ffn_conv_b, ffn_w_down):
    gw = jnp.transpose(lru_gate_w[0], (2, 3, 0, 1, 4)).reshape(
        LRU_HEADS, LRU_BLOCK, 4 * LRU_BLOCK).astype(_BF16)
    gb = (0.5 * jnp.transpose(lru_gate_b[0], (2, 0, 1, 3))).reshape(
        LRU_HEADS, 1, 4 * LRU_BLOCK)
    gb_hi = gb.astype(_BF16)
    gb_lo = (gb - gb_hi.astype(_F32)).astype(_BF16)
    gw = jnp.concatenate(
        [gw, gb_hi, gb_lo, jnp.zeros((LRU_HEADS, LRU_BLOCK - 2, 4 * LRU_BLOCK), _BF16)], axis=1)
    p = {
        "norm_mix": [norm_mix[i][None, :] for i in range(2)],
        "norm_ffn": [norm_ffn[i][None, :] for i in range(2)],
        "norm_final": norm_final[None, :],
        "w_in": lru_w_in[0].astype(_BF16),
        "lru_conv_w": 0.5 * lru_conv_w[0],
        "lru_conv_b": 0.5 * lru_conv_b[0][None, :],
        "gate_w": gw,
        "lam": lru_lambda[0],
        "w_out": lru_w_out[0].astype(_BF16),
        "w_qkv": attn_w_qkv[0].astype(_BF16),
        "bias_tab": _bias_table(attn_rpb[0]),
        "w_o": attn_w_o[0].astype(_BF16),
        "w_up": [ffn_w_up[i].astype(_BF16) for i in range(2)],
        "ffn_conv_w": [ffn_conv_w[i] for i in range(2)],
        "ffn_conv_b": [ffn_conv_b[i][None, :] for i in range(2)],
        "w_down": [ffn_w_down[i].astype(_BF16) for i in range(2)],
    }
    outs = []
    for x in (x_prompt, x_sample):
        b, seq_len, d = x.shape
        y = _trunk(x.reshape(b * seq_len, d), seq_len, p)
        outs.append(y.reshape(b, seq_len, d))
    return tuple(outs)
```

```python
import functools

import jax
import jax.numpy as jnp
from jax import lax
from jax.experimental import pallas as pl
from jax.experimental.pallas import tpu as pltpu

D_MODEL = 1024
LRU_HEADS = 8
LRU_BLOCK = 128
LRU_CONV = 4
LRU_C = 8.0
ATTN_HEADS = 16
HEAD_DIM = 64
GRID_W = 64
WIN_ROWS = 8
WIN_COLS = 16
D_FF = 2816
NORM_EPS = 1e-6

LANES = 128
SUBLANES = 8
TOKEN_TILE = 512
HALO = SUBLANES
SEG = TOKEN_TILE // SUBLANES
FF_CHUNK = 256
CONV_STRIDE = 4
ROW_BLOCK = TOKEN_TILE // GRID_W
VMEM_LIMIT = 56 * 1024 * 1024
NEG_BIG = -1e30
TINY = 1e-30
LOG2_E = 1.4426950408889634

_BF16 = jnp.bfloat16
_F32 = jnp.float32


def _const_spec(shape):
    nd = len(shape)
    return pl.BlockSpec(shape, lambda i: (0,) * nd, pipeline_mode=pl.Buffered(1))


def _rms_norm(x, g):
    ms = jnp.mean(x * x, axis=-1, keepdims=True)
    return x * lax.rsqrt(ms + NORM_EPS) * g


def _gelu_tanh(x):
    k0 = -2.0 * LOG2_E * 0.7978845608028654
    return x / (1.0 + jnp.exp2(x * ((k0 * 0.044715) * (x * x) + k0)))


def _softplus(z):
    e = jnp.exp(-jnp.abs(z))
    u = 1.0 + e
    log1p = jnp.where(u == 1.0, e, jnp.log(u) * (e / (u - 1.0)))
    return jnp.maximum(z, 0.0) + log1p


def _halo_tile(x_ref, xp_ref, xn_ref, tiles_per_seq, tile):
    t_in_seq = tile % tiles_per_seq
    keep_prev = jnp.where(t_in_seq == 0, 0.0, 1.0)
    keep_next = jnp.where(t_in_seq == tiles_per_seq - 1, 0.0, 1.0)
    return jnp.concatenate(
        [x_ref[...], xp_ref[...] * keep_prev, xn_ref[...] * keep_next], axis=0)


def _store_time_ordered(dst_ref, slot, val):
    t = TOKEN_TILE
    dst_ref[slot, HALO:HALO + t, :] = val[:t]
    dst_ref[slot, 0:HALO, :] = val[t:t + HALO]
    dst_ref[slot, HALO + t:, :] = val[t + HALO:]


def _halo_specs(n_tokens):
    per = TOKEN_TILE // HALO
    last = n_tokens // HALO - 1
    return [
        pl.BlockSpec((TOKEN_TILE, D_MODEL), lambda i: (i, 0)),
        pl.BlockSpec((HALO, D_MODEL), lambda i: (jnp.maximum(i * per - 1, 0), 0)),
        pl.BlockSpec((HALO, D_MODEL), lambda i: (jnp.minimum((i + 1) * per, last), 0)),
    ]


def _to_scan_order(dst_ref, c, val):
    for j in range(SUBLANES):
        dst_ref[c, pl.ds(j, SEG, stride=SUBLANES), :] = val[j * SEG:(j + 1) * SEG, :]


def _segment_rows(src_ref, c, j):
    return src_ref[c, pl.ds(j, SEG, stride=SUBLANES), :]


def _local_scan(sa_ref, sb_ref, hl_ref, ac_ref, reverse):
    def body(k, carry):
        s = (SEG - 1 - k) if reverse else k
        row = pl.multiple_of(s * SUBLANES, SUBLANES)
        out = []
        for c in range(LRU_HEADS):
            h, acum = carry[c]
            a = sa_ref[c, pl.ds(row, SUBLANES), :]
            b = sb_ref[c, pl.ds(row, SUBLANES), :]
            h = a * h + b
            acum = a * acum
            hl_ref[c, pl.ds(row, SUBLANES), :] = h
            ac_ref[c, pl.ds(row, SUBLANES), :] = acum
            out.append((h, acum))
        return tuple(out)

    init = tuple((jnp.zeros((SUBLANES, LANES), _F32), jnp.ones((SUBLANES, LANES), _F32))
                 for _ in range(LRU_HEADS))
    return lax.fori_loop(0, SEG, body, init, unroll=2)


def _chain_segments(h_end, a_end, carry_in, reverse):
    order = range(SUBLANES - 1, -1, -1) if reverse else range(SUBLANES)
    states = [None] * SUBLANES
    cur = carry_in
    for j in order:
        states[j] = cur
        cur = h_end[j:j + 1, :] + a_end[j:j + 1, :] * cur
    return states, cur


def _finish_scan(ends, hl_ref, ac_ref, carry_scr, c, reverse):
    states, carry = _chain_segments(ends[c][0], ends[c][1], carry_scr[c:c + 1, :], reverse)
    carry_scr[c:c + 1, :] = carry
    seg_state = jnp.concatenate(states, axis=0)[None]
    shape3 = (SEG, SUBLANES, LANES)
    fixed = hl_ref[c].reshape(shape3) + ac_ref[c].reshape(shape3) * seg_state
    return fixed.reshape(TOKEN_TILE, LANES)


def _lru_in_kernel(x_ref, xp_ref, xn_ref, g_ref, win_ref, cw_ref, cb_ref, gw_ref,
                   lam_ref, gate_out, hf_out, ab_out, bb_out,
                   rec_scr, xc_scr, sa_scr, sb_scr, hl_scr, ac_scr, carry_scr,
                   *, tiles_per_seq):
    tile = pl.program_id(0)
    t = TOKEN_TILE
    rows = t // CONV_STRIDE
    xa = _halo_tile(x_ref, xp_ref, xn_ref, tiles_per_seq, tile)
    xan = _rms_norm(xa, g_ref[...]).astype(_BF16)
    proj = jnp.dot(xan, win_ref[...], preferred_element_type=_F32)
    gate_out[...] = _gelu_tanh(proj[:t, :D_MODEL]).astype(_BF16)
    for c in range(LRU_HEADS):
        _store_time_ordered(rec_scr, c, proj[:, D_MODEL + c * LANES:D_MODEL + (c + 1) * LANES])

    @pl.when(tile % tiles_per_seq == 0)
    def _():
        carry_scr[...] = jnp.zeros_like(carry_scr)

    half_decay = (-0.5 * LRU_C * LOG2_E) * _softplus(-lam_ref[...])

    for c in range(LRU_HEADS):
        lanes = slice(c * LANES, (c + 1) * LANES)
        cw = cw_ref[:, lanes]
        cb = cb_ref[:, lanes]
        for s in range(CONV_STRIDE):
            acc = cb
            for k in range(LRU_CONV):
                win = rec_scr[c, pl.ds(HALO - 2 + s + k, rows, stride=CONV_STRIDE), :]
                acc = acc + win * cw[k:k + 1, :]
            xc_scr[c % 2, pl.ds(s, rows, stride=CONV_STRIDE), :] = acc
        half_xc = xc_scr[c % 2]
        lhs = jnp.concatenate([half_xc.astype(_BF16), jnp.ones((t, LANES), _BF16)], axis=1)
        th = jnp.tanh(jnp.dot(lhs, gw_ref[c], preferred_element_type=_F32))
        for d in range(2):
            t_r = th[:, (2 * d) * LANES:(2 * d + 1) * LANES]
            t_i = th[:, (2 * d + 1) * LANES:(2 * d + 2) * LANES]
            hd = half_decay[d:d + 1, lanes]
            a = jnp.exp2(t_r * hd + hd)
            y = 1.0 - a * a
            mult = y * lax.rsqrt(jnp.maximum(y, TINY))
            b = (t_i + 1.0) * (mult * half_xc)
            if d == 0:
                _to_scan_order(sa_scr, c, a)
                _to_scan_order(sb_scr, c, b)
            else:
                _to_scan_order(ab_out, c, a)
                _to_scan_order(bb_out, c, b)

    ends = _local_scan(sa_scr, sb_scr, hl_scr, ac_scr, reverse=False)
    for c in range(LRU_HEADS):
        hf_out[c] = _finish_scan(ends, hl_scr, ac_scr, carry_scr, c, reverse=False).astype(_BF16)


def _lru_in(x, g, w_in, conv_w, conv_b, gate_w, lam, seq_len):
    n = x.shape[0]
    nt = n // TOKEN_TILE
    t = TOKEN_TILE
    tok = pl.BlockSpec((t, D_MODEL), lambda i: (i, 0))
    gate_shape = jax.ShapeDtypeStruct((n, D_MODEL), _BF16)
    hf_shape = jax.ShapeDtypeStruct((LRU_HEADS, n, LANES), _BF16)
    ab_shape = jax.ShapeDtypeStruct((LRU_HEADS, n, LANES), _F32)
    heads = pl.BlockSpec((LRU_HEADS, t, LANES), lambda i: (0, i, 0))
    head_scr = pltpu.VMEM((LRU_HEADS, t, LANES), _F32)
    return pl.pallas_call(
        functools.partial(_lru_in_kernel, tiles_per_seq=seq_len // t),
        grid=(nt,),
        in_specs=_halo_specs(n) + [
            _const_spec((1, D_MODEL)),
            _const_spec((D_MODEL, 2 * D_MODEL)),
            _const_spec((LRU_CONV, D_MODEL)),
            _const_spec((1, D_MODEL)),
            _const_spec((LRU_HEADS, 2 * LRU_BLOCK, 4 * LRU_BLOCK)),
            _const_spec((2, D_MODEL)),
        ],
        out_specs=[tok, heads, heads, heads],
        out_shape=[gate_shape, hf_shape, ab_shape, ab_shape],
        scratch_shapes=[
            pltpu.VMEM((LRU_HEADS, t + 2 * HALO, LANES), _F32),
            pltpu.VMEM((2, t, LANES), _F32),
            head_scr, head_scr, head_scr, head_scr,
            pltpu.VMEM((LRU_HEADS, LANES), _F32),
        ],
        compiler_params=pltpu.CompilerParams(
            dimension_semantics=("arbitrary",), vmem_limit_bytes=VMEM_LIMIT),
        name="lru_in",
    )(x, x, x, g, w_in, conv_w, conv_b, gate_w, lam)


def _lru_out_kernel(x_ref, gate_ref, hf_ref, ab_ref, bb_ref, wout_ref, out_ref,
                    y_scr, hl_scr, ac_scr, sum_scr, carry_scr, *, tiles_per_seq, n_tiles):
    tile = n_tiles - 1 - pl.program_id(0)

    @pl.when(tile % tiles_per_seq == tiles_per_seq - 1)
    def _():
        carry_scr[...] = jnp.zeros_like(carry_scr)

    ends = _local_scan(ab_ref, bb_ref, hl_scr, ac_scr, reverse=True)
    for c in range(LRU_HEADS):
        lanes = slice(c * LANES, (c + 1) * LANES)
        sum_scr[c] = (hf_ref[c].astype(_F32)
                      + _finish_scan(ends, hl_scr, ac_scr, carry_scr, c, reverse=True))
        for j in range(SUBLANES):
            rows = slice(j * SEG, (j + 1) * SEG)
            h = _segment_rows(sum_scr, c, j)
            y_scr[rows, lanes] = (h * gate_ref[rows, lanes].astype(_F32)).astype(_BF16)
    out_ref[...] = x_ref[...] + jnp.dot(y_scr[...], wout_ref[...],
                                        preferred_element_type=_F32)


def _lru_out(x, gate, hf, ab, bb, w_out, seq_len):
    n = x.shape[0]
    nt = n // TOKEN_TILE
    t = TOKEN_TILE
    tok = pl.BlockSpec((t, D_MODEL), lambda i: (nt - 1 - i, 0))
    heads = pl.BlockSpec((LRU_HEADS, t, LANES), lambda i: (0, nt - 1 - i, 0))
    head_scr = pltpu.VMEM((LRU_HEADS, t, LANES), _F32)
    return pl.pallas_call(
        functools.partial(_lru_out_kernel, tiles_per_seq=seq_len // t, n_tiles=nt),
        grid=(nt,),
        in_specs=[tok, tok, heads, heads, heads, _const_spec((D_MODEL, D_MODEL))],
        out_specs=tok,
        out_shape=jax.ShapeDtypeStruct((n, D_MODEL), _F32),
        scratch_shapes=[
            pltpu.VMEM((t, D_MODEL), _BF16),
            head_scr, head_scr, head_scr,
            pltpu.VMEM((LRU_HEADS, LANES), _F32),
        ],
        compiler_params=pltpu.CompilerParams(
            dimension_semantics=("arbitrary",), vmem_limit_bytes=VMEM_LIMIT),
        name="lru_out",
    )(x, gate, hf, ab, bb, w_out)


def _ffn_kernel(x_ref, xp_ref, xn_ref, g_ref, wup_ref, cw_ref, cb_ref, wdn_ref, gfin_ref,
                out_ref, h_scr, acc_scr, *, tiles_per_seq, final_norm):
    tile = pl.program_id(0)
    t = TOKEN_TILE
    rows = t // CONV_STRIDE
    nblk = FF_CHUNK // LANES
    n_chunks = D_FF // FF_CHUNK
    xa = _halo_tile(x_ref, xp_ref, xn_ref, tiles_per_seq, tile)
    xan = _rms_norm(xa, g_ref[...]).astype(_BF16)

    def up_project(ck):
        for half in range(2):
            col0 = half * D_FF + ck * FF_CHUNK
            h = jnp.dot(xan, wup_ref[:, col0:col0 + FF_CHUNK], preferred_element_type=_F32)
            for b in range(nblk):
                _store_time_ordered(h_scr, (ck % 2) * 2 * nblk + half * nblk + b,
                                    h[:, b * LANES:(b + 1) * LANES])

    def gated_conv(ck):
        blocks = []
        for b in range(nblk):
            pieces = []
            for s in range(CONV_STRIDE):
                conv = []
                for half in range(2):
                    col = half * D_FF + ck * FF_CHUNK + b * LANES
                    cw = cw_ref[:, col:col + LANES]
                    val = cb_ref[:, col:col + LANES]
                    for k in range(3):
                        win = h_scr[(ck % 2) * 2 * nblk + half * nblk + b,
                                    pl.ds(HALO - 1 + s + k, rows, stride=CONV_STRIDE), :]
                        val = val + win * cw[k:k + 1, :]
                    conv.append(val)
                pieces.append(_gelu_tanh(conv[0]) * conv[1])
            blocks.append(jnp.concatenate(pieces, axis=0))
        return jnp.concatenate(blocks, axis=1).astype(_BF16)

    acc = jnp.zeros((t, D_MODEL), _F32)
    up_project(0)
    for ck in range(n_chunks):
        if ck + 1 < n_chunks:
            up_project(ck + 1)
        acc = acc + jnp.dot(gated_conv(ck), wdn_ref[ck * FF_CHUNK:(ck + 1) * FF_CHUNK, :],
                            preferred_element_type=_F32)
    for c in range(D_MODEL // LANES):
        for s in range(CONV_STRIDE):
            acc_scr[c, pl.ds(s, rows, stride=CONV_STRIDE), :] = (
                acc[s * rows:(s + 1) * rows, c * LANES:(c + 1) * LANES])
    y = x_ref[...] + jnp.concatenate(
        [acc_scr[c] for c in range(D_MODEL // LANES)], axis=1)
    if final_norm:
        y = _rms_norm(y, gfin_ref[...])
    out_ref[...] = y


def _conv_ffn(x, g, w_up, conv_w, conv_b, w_down, g_final, seq_len, final_norm):
    n = x.shape[0]
    t = TOKEN_TILE
    return pl.pallas_call(
        functools.partial(_ffn_kernel, tiles_per_seq=seq_len // t, final_norm=final_norm),
        grid=(n // t,),
        in_specs=_halo_specs(n) + [
            _const_spec((1, D_MODEL)),
            _const_spec((D_MODEL, 2 * D_FF)),
            _const_spec((3, 2 * D_FF)),
            _const_spec((1, 2 * D_FF)),
            _const_spec((D_FF, D_MODEL)),
            _const_spec((1, D_MODEL)),
        ],
        out_specs=pl.BlockSpec((t, D_MODEL), lambda i: (i, 0)),
        out_shape=jax.ShapeDtypeStruct((n, D_MODEL), _F32),
        scratch_shapes=[
            pltpu.VMEM((4 * FF_CHUNK // LANES, t + 2 * HALO, LANES), _F32),
            pltpu.VMEM((D_MODEL // LANES, t, LANES), _F32),
        ],
        compiler_params=pltpu.CompilerParams(
            dimension_semantics=("arbitrary",), vmem_limit_bytes=VMEM_LIMIT),
        name="conv_ffn",
    )(x, x, x, g, w_up, conv_w, conv_b, w_down, g_final)


def _qkv_kernel(x_ref, g_ref, w_ref, q_ref, k_ref, v_ref):
    xn = _rms_norm(x_ref[...], g_ref[...]).astype(_BF16)
    qkv = jnp.dot(xn, w_ref[...], preferred_element_type=_F32)
    q_ref[...] = (qkv[:, :D_MODEL] * (HEAD_DIM ** -0.5 * LOG2_E)).astype(_BF16)
    k_ref[...] = qkv[:, D_MODEL:2 * D_MODEL].astype(_BF16)
    v_ref[...] = qkv[:, 2 * D_MODEL:].astype(_BF16)


def _qkv(x, g, w_qkv):
    n = x.shape[0]
    t = TOKEN_TILE
    tok = pl.BlockSpec((t, D_MODEL), lambda i: (i, 0))
    out = jax.ShapeDtypeStruct((n, D_MODEL), _BF16)
    return pl.pallas_call(
        _qkv_kernel,
        grid=(n // t,),
        in_specs=[tok, _const_spec((1, D_MODEL)), _const_spec((D_MODEL, 3 * D_MODEL))],
        out_specs=[tok, tok, tok],
        out_shape=[out, out, out],
        compiler_params=pltpu.CompilerParams(
            dimension_semantics=("arbitrary",), vmem_limit_bytes=VMEM_LIMIT),
        name="qkv",
    )(x, g, w_qkv)


def _attn_kernel(q_ref, k_ref, v_ref, bias_ref, x_ref, wo_ref, out_ref, s_scr, o_scr,
                 *, blocks_per_seq, n_tiles):
    win = WIN_ROWS * GRID_W
    n_pairs = ATTN_HEADS // 2
    rows_per_seq = blocks_per_seq * ROW_BLOCK
    tile = pl.program_id(0)
    blk = tile % blocks_per_seq
    win_row0 = (jnp.clip(tile - 1, 0, n_tiles - 3) - (tile - blk)) * ROW_BLOCK
    lane = lax.broadcasted_iota(jnp.int32, (GRID_W, LANES), 1)
    low = lane < HEAD_DIM
    ones = jnp.ones((win, LANES), _BF16)

    def row_body(qr, carry):
        r = blk * ROW_BLOCK + qr
        rs = jnp.clip(r - WIN_ROWS // 2, 0, rows_per_seq - WIN_ROWS)
        shift = (WIN_ROWS - 1) - (r - rs)
        par = shift % 2
        m0 = shift // 2
        kstart = pl.multiple_of((rs - win_row0) * GRID_W, GRID_W)
        qrow = qr * GRID_W
        row_max = []
        for hp in range(n_pairs):
            lanes = slice(hp * LANES, (hp + 1) * LANES)
            q2 = q_ref[pl.ds(qrow, GRID_W), lanes]
            zero = jnp.zeros_like(q2)
            qs = jnp.concatenate([jnp.where(low, q2, zero), jnp.where(low, zero, q2)], axis=0)
            s = lax.dot_general(qs, k_ref[pl.ds(kstart, win), lanes], (((1,), (1,)), ((), ())),
                                preferred_element_type=_F32)
            bias = jnp.concatenate(
                [jnp.concatenate([bias_ref[par, 2 * hp + hh, m0 + m] for m in range(4)], axis=1)
                 for hh in range(2)], axis=0)
            s = s + bias
            s_scr[(qr % 2) * n_pairs + hp] = s
            row_max.append(jnp.max(s, axis=-1, keepdims=True))
        for hp in range(n_pairs):
            lanes = slice(hp * LANES, (hp + 1) * LANES)
            p = jnp.exp2(s_scr[(qr % 2) * n_pairs + hp] - row_max[hp]).astype(_BF16)
            v_ones = jnp.concatenate([v_ref[pl.ds(kstart, win), lanes], ones], axis=1)
            ov = jnp.dot(p, v_ones, preferred_element_type=_F32)
            o = ov[:, :LANES] * (1.0 / ov[:, LANES:])
            o2 = jnp.where(low, o[:GRID_W], o[GRID_W:])
            o_scr[pl.ds(qrow, GRID_W), lanes] = o2.astype(_BF16)
        return carry

    for qr in range(ROW_BLOCK):
        row_body(qr, 0)
    out_ref[...] = x_ref[...] + jnp.dot(o_scr[...], wo_ref[...], preferred_element_type=_F32)


def _attention(x, q, k, v, bias_tab, w_o, seq_len):
    n = x.shape[0]
    t = TOKEN_TILE
    nt = n // t
    tok = pl.BlockSpec((t, D_MODEL), lambda i: (i, 0))
    window = pl.BlockSpec((pl.Element(3 * t), pl.Element(D_MODEL)),
                          lambda i: (jnp.clip(i - 1, 0, nt - 3) * t, 0))
    return pl.pallas_call(
        functools.partial(_attn_kernel, blocks_per_seq=seq_len // t, n_tiles=nt),
        grid=(nt,),
        in_specs=[tok, window, window, _const_spec(bias_tab.shape), tok,
                  _const_spec((D_MODEL, D_MODEL))],
        out_specs=tok,
        out_shape=jax.ShapeDtypeStruct((n, D_MODEL), _F32),
        scratch_shapes=[
            pltpu.VMEM((ATTN_HEADS, 2 * GRID_W, WIN_ROWS * GRID_W), _F32),
            pltpu.VMEM((t, D_MODEL), _BF16),
        ],
        compiler_params=pltpu.CompilerParams(
            dimension_semantics=("arbitrary",), vmem_limit_bytes=VMEM_LIMIT),
        name="attention",
    )(q, k, v, bias_tab, x, w_o)


def _bias_table(rpb):
    col = jnp.arange(GRID_W, dtype=jnp.int32)
    col_start = jnp.clip(col - WIN_COLS // 2, 0, GRID_W - WIN_COLS)
    kc = jnp.arange(GRID_W, dtype=jnp.int32)
    valid = (kc[None, :] >= col_start[:, None]) & (kc[None, :] < col_start[:, None] + WIN_COLS)
    pad = GRID_W - WIN_COLS
    padded = jnp.pad(rpb.astype(_F32) * LOG2_E, ((0, 0), (0, 0), (pad, pad)))
    full = jnp.stack([padded[:, :, GRID_W - 1 - c:2 * GRID_W - 1 - c] for c in range(GRID_W)],
                     axis=2)
    full = jnp.where(valid[None, None], full, NEG_BIG)
    flat = jnp.transpose(full, (0, 2, 1, 3)).reshape(ATTN_HEADS, GRID_W, -1)
    flat = jnp.pad(flat, ((0, 0), (0, 0), (0, 64)), constant_values=NEG_BIG)
    tabs = []
    for par in range(2):
        sl = flat[:, :, 64 * par:64 * par + 7 * LANES]
        tabs.append(jnp.transpose(sl.reshape(ATTN_HEADS, GRID_W, 7, LANES), (0, 2, 1, 3)))
    return jnp.stack(tabs, axis=0)


def _trunk(x, seq_len, p):
    gate, hf, ab, bb = _lru_in(x, p["norm_mix"][0], p["w_in"], p["lru_conv_w"], p["lru_conv_b"],
                               p["gate_w"], p["lam"], seq_len)
    x = _lru_out(x, gate, hf, ab, bb, p["w_out"], seq_len)
    x = _conv_ffn(x, p["norm_ffn"][0], p["w_up"][0], p["ffn_conv_w"][0], p["ffn_conv_b"][0],
                  p["w_down"][0], p["norm_final"], seq_len, final_norm=False)
    q, k, v = _qkv(x, p["norm_mix"][1], p["w_qkv"])
    x = _attention(x, q, k, v, p["bias_tab"], p["w_o"], seq_len)
    x = _conv_ffn(x, p["norm_ffn"][1], p["w_up"][1], p["ffn_conv_w"][1], p["ffn_conv_b"][1],
                  p["w_down"][1], p["norm_final"], seq_len, final_norm=True)
    return x


def kernel(x_prompt, x_sample, norm_mix, norm_ffn, norm_final, lru_w_in, lru_conv_w, lru_conv_b, lru_gate_w, lru_gate_b, lru_lambda, lru_w_out, attn_w_qkv, attn_rpb, attn_w_o, ffn_w_up, ffn_conv_w, ffn_conv_b, ffn_w_down):
    gw = jnp.transpose(lru_gate_w[0], (2, 3, 0, 1, 4)).reshape(
        LRU_HEADS, LRU_BLOCK, 4 * LRU_BLOCK).astype(_BF16)
    gb = (0.5 * jnp.transpose(lru_gate_b[0], (2, 0, 1, 3))).reshape(
        LRU_HEADS, 1, 4 * LRU_BLOCK)
    gb_hi = gb.astype(_BF16)
    gb_lo = (gb - gb_hi.astype(_F32)).astype(_BF16)
    gw = jnp.concatenate(
        [gw, gb_hi, gb_lo, jnp.zeros((LRU_HEADS, LRU_BLOCK - 2, 4 * LRU_BLOCK), _BF16)], axis=1)
    p = {
        "norm_mix": [norm_mix[i][None, :] for i in range(2)],
        "norm_ffn": [norm_ffn[i][None, :] for i in range(2)],
        "norm_final": norm_final[None, :],
        "w_in": lru_w_in[0].astype(_BF16),
        "lru_conv_w": 0.5 * lru_conv_w[0],
        "lru_conv_b": 0.5 * lru_conv_b[0][None, :],
        "gate_w": gw,
        "lam": lru_lambda[0],
        "w_out": lru_w_out[0].astype(_BF16),
        "w_qkv": attn_w_qkv[0].astype(_BF16),
        "bias_tab": _bias_table(attn_rpb[0]),
        "w_o": attn_w_o[0].astype(_BF16),
        "w_up": [ffn_w_up[i].astype(_BF16) for i in range(2)],
        "ffn_conv_w": [ffn_conv_w[i] for i in range(2)],
        "ffn_conv_b": [ffn_conv_b[i][None, :] for i in range(2)],
        "w_down": [ffn_w_down[i].astype(_BF16) for i in range(2)],
    }
    outs = []
    for x in (x_prompt, x_sample):
        b, seq_len, d = x.shape
        y = _trunk(x.reshape(b * seq_len, d), seq_len, p)
        outs.append(y.reshape(b, seq_len, d))
    return tuple(outs)
```

```python
import functools

import jax
import jax.numpy as jnp
from jax import lax
from jax.experimental import pallas as pl
from jax.experimental.pallas import tpu as pltpu

D_MODEL = 1024
LRU_HEADS = 8
LRU_BLOCK = 128
LRU_CONV = 4
LRU_C = 8.0
ATTN_HEADS = 16
HEAD_DIM = 64
GRID_W = 64
WIN_ROWS = 8
WIN_COLS = 16
D_FF = 2816
NORM_EPS = 1e-6

LANES = 128
SUBLANES = 8
TOKEN_TILE = 512
HALO = SUBLANES
SEG = TOKEN_TILE // SUBLANES
FF_CHUNK = 256
CONV_STRIDE = 4
ROW_BLOCK = TOKEN_TILE // GRID_W
VMEM_LIMIT = 56 * 1024 * 1024
NEG_BIG = -1e30
TINY = 1e-30
LOG2_E = 1.4426950408889634

_BF16 = jnp.bfloat16
_F32 = jnp.float32


def _const_spec(shape):
    nd = len(shape)
    return pl.BlockSpec(shape, lambda i: (0,) * nd, pipeline_mode=pl.Buffered(1))


def _rms_norm(x, g):
    ms = jnp.mean(x * x, axis=-1, keepdims=True)
    return x * lax.rsqrt(ms + NORM_EPS) * g


def _gelu_tanh(x):
    k0 = -2.0 * LOG2_E * 0.7978845608028654
    return x / (1.0 + jnp.exp2(x * ((k0 * 0.044715) * (x * x) + k0)))


def _softplus(z):
    e = jnp.exp(-jnp.abs(z))
    u = 1.0 + e
    log1p = jnp.where(u == 1.0, e, jnp.log(u) * (e / (u - 1.0)))
    return jnp.maximum(z, 0.0) + log1p


def _halo_tile(x_ref, xp_ref, xn_ref, tiles_per_seq, tile):
    t_in_seq = tile % tiles_per_seq
    keep_prev = jnp.where(t_in_seq == 0, 0.0, 1.0)
    keep_next = jnp.where(t_in_seq == tiles_per_seq - 1, 0.0, 1.0)
    return jnp.concatenate(
        [x_ref[...], xp_ref[...] * keep_prev, xn_ref[...] * keep_next], axis=0)


def _store_time_ordered(dst_ref, slot, val):
    t = TOKEN_TILE
    dst_ref[slot, HALO:HALO + t, :] = val[:t]
    dst_ref[slot, 0:HALO, :] = val[t:t + HALO]
    dst_ref[slot, HALO + t:, :] = val[t + HALO:]


def _halo_specs(n_tokens):
    per = TOKEN_TILE // HALO
    last = n_tokens // HALO - 1
    return [
        pl.BlockSpec((TOKEN_TILE, D_MODEL), lambda i: (i, 0)),
        pl.BlockSpec((HALO, D_MODEL), lambda i: (jnp.maximum(i * per - 1, 0), 0)),
        pl.BlockSpec((HALO, D_MODEL), lambda i: (jnp.minimum((i + 1) * per, last), 0)),
    ]


def _to_scan_order(dst_ref, c, val):
    for j in range(SUBLANES):
        dst_ref[c, pl.ds(j, SEG, stride=SUBLANES), :] = val[j * SEG:(j + 1) * SEG, :]


def _segment_rows(src_ref, c, j):
    return src_ref[c, pl.ds(j, SEG, stride=SUBLANES), :]


def _local_scan(sa_ref, sb_ref, hl_ref, ac_ref, reverse):
    def body(k, carry):
        s = (SEG - 1 - k) if reverse else k
        row = pl.multiple_of(s * SUBLANES, SUBLANES)
        out = []
        for c in range(LRU_HEADS):
            h, acum = carry[c]
            a = sa_ref[c, pl.ds(row, SUBLANES), :]
            b = sb_ref[c, pl.ds(row, SUBLANES), :]
            h = a * h + b
            acum = a * acum
            hl_ref[c, pl.ds(row, SUBLANES), :] = h
            ac_ref[c, pl.ds(row, SUBLANES), :] = acum
            out.append((h, acum))
        return tuple(out)

    init = tuple((jnp.zeros((SUBLANES, LANES), _F32), jnp.ones((SUBLANES, LANES), _F32))
                 for _ in range(LRU_HEADS))
    return lax.fori_loop(0, SEG, body, init, unroll=2)


def _chain_segments(h_end, a_end, carry_in, reverse):
    order = range(SUBLANES - 1, -1, -1) if reverse else range(SUBLANES)
    states = [None] * SUBLANES
    cur = carry_in
    for j in order:
        states[j] = cur
        cur = h_end[j:j + 1, :] + a_end[j:j + 1, :] * cur
    return states, cur


def _finish_scan(ends, hl_ref, ac_ref, carry_scr, c, reverse):
    states, carry = _chain_segments(ends[c][0], ends[c][1], carry_scr[c:c + 1, :], reverse)
    carry_scr[c:c + 1, :] = carry
    seg_state = jnp.concatenate(states, axis=0)[None]
    shape3 = (SEG, SUBLANES, LANES)
    fixed = hl_ref[c].reshape(shape3) + ac_ref[c].reshape(shape3) * seg_state
    return fixed.reshape(TOKEN_TILE, LANES)


def _lru_in_kernel(x_ref, xp_ref, xn_ref, g_ref, win_ref, cw_ref, cb_ref, gw_ref,
                   lam_ref, gate_out, hf_out, ab_out, bb_out,
                   rec_scr, xc_scr, sa_scr, sb_scr, hl_scr, ac_scr, carry_scr,
                   *, tiles_per_seq):
    tile = pl.program_id(0)
    t = TOKEN_TILE
    rows = t // CONV_STRIDE
    xa = _halo_tile(x_ref, xp_ref, xn_ref, tiles_per_seq, tile)
    xan = _rms_norm(xa, g_ref[...]).astype(_BF16)
    proj = jnp.dot(xan, win_ref[...], preferred_element_type=_F32)
    gate_out[...] = _gelu_tanh(proj[:t, :D_MODEL]).astype(_BF16)
    for c in range(LRU_HEADS):
        _store_time_ordered(rec_scr, c, proj[:, D_MODEL + c * LANES:D_MODEL + (c + 1) * LANES])

    @pl.when(tile % tiles_per_seq == 0)
    def _():
        carry_scr[...] = jnp.zeros_like(carry_scr)

    half_decay = (-0.5 * LRU_C * LOG2_E) * _softplus(-lam_ref[...])

    for c in range(LRU_HEADS):
        lanes = slice(c * LANES, (c + 1) * LANES)
        cw = cw_ref[:, lanes]
        cb = cb_ref[:, lanes]
        for s in range(CONV_STRIDE):
            acc = cb
            for k in range(LRU_CONV):
                win = rec_scr[c, pl.ds(HALO - 2 + s + k, rows, stride=CONV_STRIDE), :]
                acc = acc + win * cw[k:k + 1, :]
            xc_scr[c % 2, pl.ds(s, rows, stride=CONV_STRIDE), :] = acc
        half_xc = xc_scr[c % 2]
        lhs = jnp.concatenate([half_xc.astype(_BF16), jnp.ones((t, LANES), _BF16)], axis=1)
        th = jnp.tanh(jnp.dot(lhs, gw_ref[c], preferred_element_type=_F32))
        for d in range(2):
            t_r = th[:, (2 * d) * LANES:(2 * d + 1) * LANES]
            t_i = th[:, (2 * d + 1) * LANES:(2 * d + 2) * LANES]
            hd = half_decay[d:d + 1, lanes]
            a = jnp.exp2(t_r * hd + hd)
            y = 1.0 - a * a
            mult = y * lax.rsqrt(jnp.maximum(y, TINY))
            b = (t_i + 1.0) * (mult * half_xc)
            if d == 0:
                _to_scan_order(sa_scr, c, a)
                _to_scan_order(sb_scr, c, b)
            else:
                _to_scan_order(ab_out, c, a)
                _to_scan_order(bb_out, c, b)

    ends = _local_scan(sa_scr, sb_scr, hl_scr, ac_scr, reverse=False)
    for c in range(LRU_HEADS):
        hf_out[c] = _finish_scan(ends, hl_scr, ac_scr, carry_scr, c, reverse=False).astype(_BF16)


def _lru_in(x, g, w_in, conv_w, conv_b, gate_w, lam, seq_len):
    n = x.shape[0]
    nt = n // TOKEN_TILE
    t = TOKEN_TILE
    tok = pl.BlockSpec((t, D_MODEL), lambda i: (i, 0))
    gate_shape = jax.ShapeDtypeStruct((n, D_MODEL), _BF16)
    hf_shape = jax.ShapeDtypeStruct((LRU_HEADS, n, LANES), _BF16)
    ab_shape = jax.ShapeDtypeStruct((LRU_HEADS, n, LANES), _F32)
    heads = pl.BlockSpec((LRU_HEADS, t, LANES), lambda i: (0, i, 0))
    head_scr = pltpu.VMEM((LRU_HEADS, t, LANES), _F32)
    return pl.pallas_call(
        functools.partial(_lru_in_kernel, tiles_per_seq=seq_len // t),
        grid=(nt,),
        in_specs=_halo_specs(n) + [
            _const_spec((1, D_MODEL)),
            _const_spec((D_MODEL, 2 * D_MODEL)),
            _const_spec((LRU_CONV, D_MODEL)),
            _const_spec((1, D_MODEL)),
            _const_spec((LRU_HEADS, 2 * LRU_BLOCK, 4 * LRU_BLOCK)),
            _const_spec((2, D_MODEL)),
        ],
        out_specs=[tok, heads, heads, heads],
        out_shape=[gate_shape, hf_shape, ab_shape, ab_shape],
        scratch_shapes=[
            pltpu.VMEM((LRU_HEADS, t + 2 * HALO, LANES), _F32),
            pltpu.VMEM((2, t, LANES), _F32),
            head_scr, head_scr, head_scr, head_scr,
            pltpu.VMEM((LRU_HEADS, LANES), _F32),
        ],
        compiler_params=pltpu.CompilerParams(
            dimension_semantics=("arbitrary",), vmem_limit_bytes=VMEM_LIMIT),
        name="lru_in",
    )(x, x, x, g, w_in, conv_w, conv_b, gate_w, lam)


def _lru_out_kernel(x_ref, gate_ref, hf_ref, ab_ref, bb_ref, wout_ref, out_ref,
                    y_scr, hl_scr, ac_scr, sum_scr, carry_scr, *, tiles_per_seq, n_tiles):
    tile = n_tiles - 1 - pl.program_id(0)

    @pl.when(tile % tiles_per_seq == tiles_per_seq - 1)
    def _():
        carry_scr[...] = jnp.zeros_like(carry_scr)

    ends = _local_scan(ab_ref, bb_ref, hl_scr, ac_scr, reverse=True)
    for c in range(LRU_HEADS):
        lanes = slice(c * LANES, (c + 1) * LANES)
        sum_scr[c] = (hf_ref[c].astype(_F32)
                      + _finish_scan(ends, hl_scr, ac_scr, carry_scr, c, reverse=True))
        for j in range(SUBLANES):
            rows = slice(j * SEG, (j + 1) * SEG)
            h = _segment_rows(sum_scr, c, j)
            y_scr[rows, lanes] = (h * gate_ref[rows, lanes].astype(_F32)).astype(_BF16)
    out_ref[...] = x_ref[...] + jnp.dot(y_scr[...], wout_ref[...],
                                        preferred_element_type=_F32)


def _lru_out(x, gate, hf, ab, bb, w_out, seq_len):
    n = x.shape[0]
    nt = n // TOKEN_TILE
    t = TOKEN_TILE
    tok = pl.BlockSpec((t, D_MODEL), lambda i: (nt - 1 - i, 0))
    heads = pl.BlockSpec((LRU_HEADS, t, LANES), lambda i: (0, nt - 1 - i, 0))
    head_scr = pltpu.VMEM((LRU_HEADS, t, LANES), _F32)
    return pl.pallas_call(
        functools.partial(_lru_out_kernel, tiles_per_seq=seq_len // t, n_tiles=nt),
        grid=(nt,),
        in_specs=[tok, tok, heads, heads, heads, _const_spec((D_MODEL, D_MODEL))],
        out_specs=tok,
        out_shape=jax.ShapeDtypeStruct((n, D_MODEL), _F32),
        scratch_shapes=[
            pltpu.VMEM((t, D_MODEL), _BF16),
            head_scr, head_scr, head_scr,
            pltpu.VMEM((LRU_HEADS, LANES), _F32),
        ],
        compiler_params=pltpu.CompilerParams(
            dimension_semantics=("arbitrary",), vmem_limit_bytes=VMEM_LIMIT),
        name="lru_out",
    )(x, gate, hf, ab, bb, w_out)


def _ffn_kernel(x_ref, xp_ref, xn_ref, g_ref, wup_ref, cw_ref, cb_ref, wdn_ref, gfin_ref,
                out_ref, h_scr, acc_scr, *, tiles_per_seq, final_norm):
    tile = pl.program_id(0)
    t = TOKEN_TILE
    rows = t // CONV_STRIDE
    nblk = FF_CHUNK // LANES
    n_chunks = D_FF // FF_CHUNK
    xa = _halo_tile(x_ref, xp_ref, xn_ref, tiles_per_seq, tile)
    xan = _rms_norm(xa, g_ref[...]).astype(_BF16)

    def up_project(ck):
        for half in range(2):
            col0 = half * D_FF + ck * FF_CHUNK
            h = jnp.dot(xan, wup_ref[:, col0:col0 + FF_CHUNK], preferred_element_type=_F32)
            for b in range(nblk):
                _store_time_ordered(h_scr, (ck % 2) * 2 * nblk + half * nblk + b,
                                    h[:, b * LANES:(b + 1) * LANES])

    def gated_conv(ck):
        blocks = []
        for b in range(nblk):
            pieces = []
            for s in range(CONV_STRIDE):
                conv = []
                for half in range(2):
                    col = half * D_FF + ck * FF_CHUNK + b * LANES
                    cw = cw_ref[:, col:col + LANES].astype(_BF16)
                    val = cb_ref[:, col:col + LANES].astype(_BF16)
                    for k in range(3):
                        win = h_scr[(ck % 2) * 2 * nblk + half * nblk + b,
                                    pl.ds(HALO - 1 + s + k, rows, stride=CONV_STRIDE), :]
                        val = val + win.astype(_BF16) * cw[k:k + 1, :]
                    conv.append(val)
                pieces.append(_gelu_tanh(conv[0]) * conv[1])
            blocks.append(jnp.concatenate(pieces, axis=0))
        return jnp.concatenate(blocks, axis=1)

    acc = jnp.zeros((t, D_MODEL), _F32)
    up_project(0)
    for ck in range(n_chunks):
        if ck + 1 < n_chunks:
            up_project(ck + 1)
        acc = acc + jnp.dot(gated_conv(ck), wdn_ref[ck * FF_CHUNK:(ck + 1) * FF_CHUNK, :],
                            preferred_element_type=_F32)
    for c in range(D_MODEL // LANES):
        for s in range(CONV_STRIDE):
            acc_scr[c, pl.ds(s, rows, stride=CONV_STRIDE), :] = (
                acc[s * rows:(s + 1) * rows, c * LANES:(c + 1) * LANES])
    y = x_ref[...] + jnp.concatenate(
        [acc_scr[c] for c in range(D_MODEL // LANES)], axis=1)
    if final_norm:
        y = _rms_norm(y, gfin_ref[...])
    out_ref[...] = y


def _conv_ffn(x, g, w_up, conv_w, conv_b, w_down, g_final, seq_len, final_norm):
    n = x.shape[0]
    t = TOKEN_TILE
    return pl.pallas_call(
        functools.partial(_ffn_kernel, tiles_per_seq=seq_len // t, final_norm=final_norm),
        grid=(n // t,),
        in_specs=_halo_specs(n) + [
            _const_spec((1, D_MODEL)),
            _const_spec((D_MODEL, 2 * D_FF)),
            _const_spec((3, 2 * D_FF)),
            _const_spec((1, 2 * D_FF)),
            _const_spec((D_FF, D_MODEL)),
            _const_spec((1, D_MODEL)),
        ],
        out_specs=pl.BlockSpec((t, D_MODEL), lambda i: (i, 0)),
        out_shape=jax.ShapeDtypeStruct((n, D_MODEL), _F32),
        scratch_shapes=[
            pltpu.VMEM((4 * FF_CHUNK // LANES, t + 2 * HALO, LANES), _F32),
            pltpu.VMEM((D_MODEL // LANES, t, LANES), _F32),
        ],
        compiler_params=pltpu.CompilerParams(
            dimension_semantics=("arbitrary",), vmem_limit_bytes=VMEM_LIMIT),
        name="conv_ffn",
    )(x, x, x, g, w_up, conv_w, conv_b, w_down, g_final)


def _qkv_kernel(x_ref, g_ref, w_ref, q_ref, k_ref, v_ref):
    xn = _rms_norm(x_ref[...], g_ref[...]).astype(_BF16)
    qkv = jnp.dot(xn, w_ref[...], preferred_element_type=_F32)
    q_ref[...] = (qkv[:, :D_MODEL] * (HEAD_DIM ** -0.5 * LOG2_E)).astype(_BF16)
    k_ref[...] = qkv[:, D_MODEL:2 * D_MODEL].astype(_BF16)
    v_ref[...] = qkv[:, 2 * D_MODEL:].astype(_BF16)


def _qkv(x, g, w_qkv):
    n = x.shape[0]
    t = TOKEN_TILE
    tok = pl.BlockSpec((t, D_MODEL), lambda i: (i, 0))
    out = jax.ShapeDtypeStruct((n, D_MODEL), _BF16)
    return pl.pallas_call(
        _qkv_kernel,
        grid=(n // t,),
        in_specs=[tok, _const_spec((1, D_MODEL)), _const_spec((D_MODEL, 3 * D_MODEL))],
        out_specs=[tok, tok, tok],
        out_shape=[out, out, out],
        compiler_params=pltpu.CompilerParams(
            dimension_semantics=("arbitrary",), vmem_limit_bytes=VMEM_LIMIT),
        name="qkv",
    )(x, g, w_qkv)


def _attn_kernel(q_ref, k_ref, v_ref, bias_ref, x_ref, wo_ref, out_ref, s_scr, o_scr,
                 *, blocks_per_seq, n_tiles):
    win = WIN_ROWS * GRID_W
    n_pairs = ATTN_HEADS // 2
    rows_per_seq = blocks_per_seq * ROW_BLOCK
    tile = pl.program_id(0)
    blk = tile % blocks_per_seq
    win_row0 = (jnp.clip(tile - 1, 0, n_tiles - 3) - (tile - blk)) * ROW_BLOCK
    lane = lax.broadcasted_iota(jnp.int32, (GRID_W, LANES), 1)
    low = lane < HEAD_DIM
    ones = jnp.ones((win, LANES), _BF16)

    def row_body(qr, carry):
        r = blk * ROW_BLOCK + qr
        rs = jnp.clip(r - WIN_ROWS // 2, 0, rows_per_seq - WIN_ROWS)
        shift = (WIN_ROWS - 1) - (r - rs)
        par = shift % 2
        m0 = shift // 2
        kstart = pl.multiple_of((rs - win_row0) * GRID_W, GRID_W)
        qrow = qr * GRID_W
        row_max = []
        for hp in range(n_pairs):
            lanes = slice(hp * LANES, (hp + 1) * LANES)
            q2 = q_ref[pl.ds(qrow, GRID_W), lanes]
            zero = jnp.zeros_like(q2)
            qs = jnp.concatenate([jnp.where(low, q2, zero), jnp.where(low, zero, q2)], axis=0)
            s = lax.dot_general(qs, k_ref[pl.ds(kstart, win), lanes], (((1,), (1,)), ((), ())),
                                preferred_element_type=_F32)
            bias = jnp.concatenate(
                [jnp.concatenate([bias_ref[par, 2 * hp + hh, m0 + m] for m in range(4)], axis=1)
                 for hh in range(2)], axis=0)
            s = s + bias
            s_scr[(qr % 2) * n_pairs + hp] = s
            row_max.append(jnp.max(s, axis=-1, keepdims=True))
        for hp in range(n_pairs):
            lanes = slice(hp * LANES, (hp + 1) * LANES)
            p = jnp.exp2(s_scr[(qr % 2) * n_pairs + hp] - row_max[hp]).astype(_BF16)
            v_ones = jnp.concatenate([v_ref[pl.ds(kstart, win), lanes], ones], axis=1)
            ov = jnp.dot(p, v_ones, preferred_element_type=_F32)
            o = ov[:, :LANES] * (1.0 / ov[:, LANES:])
            o2 = jnp.where(low, o[:GRID_W], o[GRID_W:])
            o_scr[pl.ds(qrow, GRID_W), lanes] = o2.astype(_BF16)
        return carry

    for qr in range(ROW_BLOCK):
        row_body(qr, 0)
    out_ref[...] = x_ref[...] + jnp.dot(o_scr[...], wo_ref[...], preferred_element_type=_F32)


def _attention(x, q, k, v, bias_tab, w_o, seq_len):
    n = x.shape[0]
    t = TOKEN_TILE
    nt = n // t
    tok = pl.BlockSpec((t, D_MODEL), lambda i: (i, 0))
    window = pl.BlockSpec((pl.Element(3 * t), pl.Element(D_MODEL)),
                          lambda i: (jnp.clip(i - 1, 0, nt - 3) * t, 0))
    return pl.pallas_call(
        functools.partial(_attn_kernel, blocks_per_seq=seq_len // t, n_tiles=nt),
        grid=(nt,),
        in_specs=[tok, window, window, _const_spec(bias_tab.shape), tok,
                  _const_spec((D_MODEL, D_MODEL))],
        out_specs=tok,
        out_shape=jax.ShapeDtypeStruct((n, D_MODEL), _F32),
        scratch_shapes=[
            pltpu.VMEM((ATTN_HEADS, 2 * GRID_W, WIN_ROWS * GRID_W), _F32),
            pltpu.VMEM((t, D_MODEL), _BF16),
        ],
        compiler_params=pltpu.CompilerParams(
            dimension_semantics=("arbitrary",), vmem_limit_bytes=VMEM_LIMIT),
        name="attention",
    )(q, k, v, bias_tab, x, w_o)


def _bias_table(rpb):
    col = jnp.arange(GRID_W, dtype=jnp.int32)
    col_start = jnp.clip(col - WIN_COLS // 2, 0, GRID_W - WIN_COLS)
    kc = jnp.arange(GRID_W, dtype=jnp.int32)
    valid = (kc[None, :] >= col_start[:, None]) & (kc[None, :] < col_start[:, None] + WIN_COLS)
    pad = GRID_W - WIN_COLS
    padded = jnp.pad(rpb.astype(_F32) * LOG2_E, ((0, 0), (0, 0), (pad, pad)))
    full = jnp.stack([padded[:, :, GRID_W - 1 - c:2 * GRID_W - 1 - c] for c in range(GRID_W)],
                     axis=2)
    full = jnp.where(valid[None, None], full, NEG_BIG)
    flat = jnp.transpose(full, (0, 2, 1, 3)).reshape(ATTN_HEADS, GRID_W, -1)
    flat = jnp.pad(flat, ((0, 0), (0, 0), (0, 64)), constant_values=NEG_BIG)
    tabs = []
    for par in range(2):
        sl = flat[:, :, 64 * par:64 * par + 7 * LANES]
        tabs.append(jnp.transpose(sl.reshape(ATTN_HEADS, GRID_W, 7, LANES), (0, 2, 1, 3)))
    return jnp.stack(tabs, axis=0)


def _trunk(x, seq_len, p):
    gate, hf, ab, bb = _lru_in(x, p["norm_mix"][0], p["w_in"], p["lru_conv_w"], p["lru_conv_b"],
                               p["gate_w"], p["lam"], seq_len)
    x = _lru_out(x, gate, hf, ab, bb, p["w_out"], seq_len)
    x = _conv_ffn(x, p["norm_ffn"][0], p["w_up"][0], p["ffn_conv_w"][0], p["ffn_conv_b"][0],
                  p["w_down"][0], p["norm_final"], seq_len, final_norm=False)
    q, k, v = _qkv(x, p["norm_mix"][1], p["w_qkv"])
    x = _attention(x, q, k, v, p["bias_tab"], p["w_o"], seq_len)
    x = _conv_ffn(x, p["norm_ffn"][1], p["w_up"][1], p["ffn_conv_w"][1], p["ffn_conv_b"][1],
                  p["w_down"][1], p["norm_final"], seq_len, final_norm=True)
    return x


def kernel(x_prompt, x_sample, norm_mix, norm_ffn, norm_final, lru_w_in, lru_conv_w, lru_conv_b, lru_gate_w, lru_gate_b, lru_lambda, lru_w_out, attn_w_qkv, attn_rpb, attn_w_o, ffn_w_up, ffn_conv_w, ffn_conv_b, ffn_w_down):
    gw = jnp.transpose(lru_gate_w[0], (2, 3, 0, 1, 4)).reshape(
        LRU_HEADS, LRU_BLOCK, 4 * LRU_BLOCK).astype(_BF16)
    gb = (0.5 * jnp.transpose(lru_gate_b[0], (2, 0, 1, 3))).reshape(
        LRU_HEADS, 1, 4 * LRU_BLOCK)
    gb_hi = gb.astype(_BF16)
    gb_lo = (gb - gb_hi.astype(_F32)).astype(_BF16)
    gw = jnp.concatenate(
        [gw, gb_hi, gb_lo, jnp.zeros((LRU_HEADS, LRU_BLOCK - 2, 4 * LRU_BLOCK), _BF16)], axis=1)
    p = {
        "norm_mix": [norm_mix[i][None, :] for i in range(2)],
        "norm_ffn": [norm_ffn[i][None, :] for i in range(2)],
        "norm_final": norm_final[None, :],
        "w_in": lru_w_in[0].astype(_BF16),
        "lru_conv_w": 0.5 * lru_conv_w[0],
        "lru_conv_b": 0.5 * lru_conv_b[0][None, :],
        "gate_w": gw,
        "lam": lru_lambda[0],
        "w_out": lru_w_out[0].astype(_BF16),
        "w_qkv": attn_w_qkv[0].astype(_BF16),
        "bias_tab": _bias_table(attn_rpb[0]),
        "w_o": attn_w_o[0].astype(_BF16),
        "w_up": [ffn_w_up[i].astype(_BF16) for i in range(2)],
        "ffn_conv_w": [ffn_conv_w[i] for i in range(2)],
        "ffn_conv_b": [ffn_conv_b[i][None, :] for i in range(2)],
        "w_down": [ffn_w_down[i].astype(_BF16) for i in range(2)],
    }
    outs = []
    for x in (x_prompt, x_sample):
        b, seq_len, d = x.shape
        y = _trunk(x.reshape(b * seq_len, d), seq_len, p)
        outs.append(y.reshape(b, seq_len, d))
    return tuple(outs)
```

```python
import functools

import jax
import jax.numpy as jnp
from jax import lax
from jax.experimental import pallas as pl
from jax.experimental.pallas import tpu as pltpu

D_MODEL = 1024
LRU_HEADS = 8
LRU_BLOCK = 128
LRU_CONV = 4
LRU_C = 8.0
ATTN_HEADS = 16
HEAD_DIM = 64
GRID_W = 64
WIN_ROWS = 8
WIN_COLS = 16
D_FF = 2816
NORM_EPS = 1e-6

LANES = 128
SUBLANES = 8
TOKEN_TILE = 512
HALO = SUBLANES
SEG = TOKEN_TILE // SUBLANES
FF_CHUNK = 256
CONV_STRIDE = 4
ROW_BLOCK = TOKEN_TILE // GRID_W
VMEM_LIMIT = 56 * 1024 * 1024
NEG_BIG = -1e30
TINY = 1e-30
LOG2_E = 1.4426950408889634

_BF16 = jnp.bfloat16
_F32 = jnp.float32


def _const_spec(shape):
    nd = len(shape)
    return pl.BlockSpec(shape, lambda i: (0,) * nd, pipeline_mode=pl.Buffered(1))


def _rms_norm(x, g):
    ms = jnp.mean(x * x, axis=-1, keepdims=True)
    return x * lax.rsqrt(ms + NORM_EPS) * g


def _gelu_tanh(x):
    k0 = -2.0 * LOG2_E * 0.7978845608028654
    return x / (1.0 + jnp.exp2(x * ((k0 * 0.044715) * (x * x) + k0)))


def _softplus(z):
    e = jnp.exp(-jnp.abs(z))
    u = 1.0 + e
    log1p = jnp.where(u == 1.0, e, jnp.log(u) * (e / (u - 1.0)))
    return jnp.maximum(z, 0.0) + log1p


def _halo_tile(x_ref, xp_ref, xn_ref, tiles_per_seq, tile):
    t_in_seq = tile % tiles_per_seq
    keep_prev = jnp.where(t_in_seq == 0, 0.0, 1.0)
    keep_next = jnp.where(t_in_seq == tiles_per_seq - 1, 0.0, 1.0)
    return jnp.concatenate(
        [x_ref[...], xp_ref[...] * keep_prev, xn_ref[...] * keep_next], axis=0)


def _store_time_ordered(dst_ref, slot, val):
    t = TOKEN_TILE
    dst_ref[slot, HALO:HALO + t, :] = val[:t]
    dst_ref[slot, 0:HALO, :] = val[t:t + HALO]
    dst_ref[slot, HALO + t:, :] = val[t + HALO:]


def _halo_specs(n_tokens):
    per = TOKEN_TILE // HALO
    last = n_tokens // HALO - 1
    return [
        pl.BlockSpec((TOKEN_TILE, D_MODEL), lambda i: (i, 0)),
        pl.BlockSpec((HALO, D_MODEL), lambda i: (jnp.maximum(i * per - 1, 0), 0)),
        pl.BlockSpec((HALO, D_MODEL), lambda i: (jnp.minimum((i + 1) * per, last), 0)),
    ]


def _to_scan_order(dst_ref, c, val):
    for j in range(SUBLANES):
        dst_ref[c, pl.ds(j, SEG, stride=SUBLANES), :] = val[j * SEG:(j + 1) * SEG, :]


def _segment_rows(src_ref, c, j):
    return src_ref[c, pl.ds(j, SEG, stride=SUBLANES), :]


def _local_scan(sa_ref, sb_ref, hl_ref, ac_ref, reverse):
    def body(k, carry):
        s = (SEG - 1 - k) if reverse else k
        row = pl.multiple_of(s * SUBLANES, SUBLANES)
        out = []
        for c in range(LRU_HEADS):
            h, acum = carry[c]
            a = sa_ref[c, pl.ds(row, SUBLANES), :]
            b = sb_ref[c, pl.ds(row, SUBLANES), :]
            h = a * h + b
            acum = a * acum
            hl_ref[c, pl.ds(row, SUBLANES), :] = h
            ac_ref[c, pl.ds(row, SUBLANES), :] = acum
            out.append((h, acum))
        return tuple(out)

    init = tuple((jnp.zeros((SUBLANES, LANES), _F32), jnp.ones((SUBLANES, LANES), _F32))
                 for _ in range(LRU_HEADS))
    return lax.fori_loop(0, SEG, body, init, unroll=2)


def _chain_segments(h_end, a_end, carry_in, reverse):
    order = range(SUBLANES - 1, -1, -1) if reverse else range(SUBLANES)
    states = [None] * SUBLANES
    cur = carry_in
    for j in order:
        states[j] = cur
        cur = h_end[j:j + 1, :] + a_end[j:j + 1, :] * cur
    return states, cur


def _finish_scan(ends, hl_ref, ac_ref, carry_scr, c, reverse):
    states, carry = _chain_segments(ends[c][0], ends[c][1], carry_scr[c:c + 1, :], reverse)
    carry_scr[c:c + 1, :] = carry
    seg_state = jnp.concatenate(states, axis=0)[None]
    shape3 = (SEG, SUBLANES, LANES)
    fixed = hl_ref[c].reshape(shape3) + ac_ref[c].reshape(shape3) * seg_state
    return fixed.reshape(TOKEN_TILE, LANES)


def _lru_proj_kernel(x_ref, g_ref, win_ref, gate_out, rec_out):
    xn = _rms_norm(x_ref[...], g_ref[...]).astype(_BF16)
    proj = jnp.dot(xn, win_ref[...], preferred_element_type=_F32)
    gate_out[...] = _gelu_tanh(proj[:, :D_MODEL]).astype(_BF16)
    rec_out[...] = proj[:, D_MODEL:]


def _lru_proj(x, g, w_in):
    n = x.shape[0]
    t = TOKEN_TILE
    tok = pl.BlockSpec((t, D_MODEL), lambda i: (i, 0))
    return pl.pallas_call(
        _lru_proj_kernel,
        grid=(n // t,),
        in_specs=[tok, _const_spec((1, D_MODEL)), _const_spec((D_MODEL, 2 * D_MODEL))],
        out_specs=[tok, tok],
        out_shape=[jax.ShapeDtypeStruct((n, D_MODEL), _BF16),
                   jax.ShapeDtypeStruct((n, D_MODEL), _F32)],
        compiler_params=pltpu.CompilerParams(
            dimension_semantics=("arbitrary",), vmem_limit_bytes=VMEM_LIMIT),
        name="lru_proj",
    )(x, g, w_in)


def _lru_in_kernel(rec_ref, recp_ref, recn_ref, cw_ref, cb_ref, gw_ref, lam_ref,
                   hf_out, ab_out, bb_out,
                   rec_scr, xc_scr, sa_scr, sb_scr, hl_scr, ac_scr, carry_scr,
                   *, tiles_per_seq):
    tile = pl.program_id(0)
    t = TOKEN_TILE
    rows = t // CONV_STRIDE
    t_in_seq = tile % tiles_per_seq
    keep_prev = jnp.where(t_in_seq == 0, 0.0, 1.0)
    keep_next = jnp.where(t_in_seq == tiles_per_seq - 1, 0.0, 1.0)
    for c in range(LRU_HEADS):
        lanes = slice(c * LANES, (c + 1) * LANES)
        rec_scr[c, 0:HALO, :] = recp_ref[:, lanes] * keep_prev
        rec_scr[c, HALO:HALO + t, :] = rec_ref[:, lanes]
        rec_scr[c, HALO + t:, :] = recn_ref[:, lanes] * keep_next

    @pl.when(t_in_seq == 0)
    def _():
        carry_scr[...] = jnp.zeros_like(carry_scr)

    half_decay = (-0.5 * LRU_C * LOG2_E) * _softplus(-lam_ref[...])

    for c in range(LRU_HEADS):
        lanes = slice(c * LANES, (c + 1) * LANES)
        cw = cw_ref[:, lanes]
        cb = cb_ref[:, lanes]
        for s in range(CONV_STRIDE):
            acc = cb
            for k in range(LRU_CONV):
                win = rec_scr[c, pl.ds(HALO - 2 + s + k, rows, stride=CONV_STRIDE), :]
                acc = acc + win * cw[k:k + 1, :]
            xc_scr[c % 2, pl.ds(s, rows, stride=CONV_STRIDE), :] = acc
        half_xc = xc_scr[c % 2]
        lhs = jnp.concatenate([half_xc.astype(_BF16), jnp.ones((t, LANES), _BF16)], axis=1)
        th = jnp.tanh(jnp.dot(lhs, gw_ref[c], preferred_element_type=_F32))
        for d in range(2):
            t_r = th[:, (2 * d) * LANES:(2 * d + 1) * LANES]
            t_i = th[:, (2 * d + 1) * LANES:(2 * d + 2) * LANES]
            hd = half_decay[d:d + 1, lanes]
            a = jnp.exp2(t_r * hd + hd)
            y = 1.0 - a * a
            mult = y * lax.rsqrt(jnp.maximum(y, TINY))
            b = (t_i + 1.0) * (mult * half_xc)
            if d == 0:
                _to_scan_order(sa_scr, c, a)
                _to_scan_order(sb_scr, c, b)
            else:
                _to_scan_order(ab_out, c, a)
                _to_scan_order(bb_out, c, b)

    ends = _local_scan(sa_scr, sb_scr, hl_scr, ac_scr, reverse=False)
    for c in range(LRU_HEADS):
        hf_out[c] = _finish_scan(ends, hl_scr, ac_scr, carry_scr, c, reverse=False).astype(_BF16)


def _lru_in(rec, conv_w, conv_b, gate_w, lam, seq_len):
    n = rec.shape[0]
    nt = n // TOKEN_TILE
    t = TOKEN_TILE
    hf_shape = jax.ShapeDtypeStruct((LRU_HEADS, n, LANES), _BF16)
    ab_shape = jax.ShapeDtypeStruct((LRU_HEADS, n, LANES), _F32)
    heads = pl.BlockSpec((LRU_HEADS, t, LANES), lambda i: (0, i, 0))
    head_scr = pltpu.VMEM((LRU_HEADS, t, LANES), _F32)
    return pl.pallas_call(
        functools.partial(_lru_in_kernel, tiles_per_seq=seq_len // t),
        grid=(nt,),
        in_specs=_halo_specs(n) + [
            _const_spec((LRU_CONV, D_MODEL)),
            _const_spec((1, D_MODEL)),
            _const_spec((LRU_HEADS, 2 * LRU_BLOCK, 4 * LRU_BLOCK)),
            _const_spec((2, D_MODEL)),
        ],
        out_specs=[heads, heads, heads],
        out_shape=[hf_shape, ab_shape, ab_shape],
        scratch_shapes=[
            pltpu.VMEM((LRU_HEADS, t + 2 * HALO, LANES), _F32),
            pltpu.VMEM((2, t, LANES), _F32),
            head_scr, head_scr, head_scr, head_scr,
            pltpu.VMEM((LRU_HEADS, LANES), _F32),
        ],
        compiler_params=pltpu.CompilerParams(
            dimension_semantics=("arbitrary",), vmem_limit_bytes=VMEM_LIMIT),
        name="lru_in",
    )(rec, rec, rec, conv_w, conv_b, gate_w, lam)


def _lru_out_kernel(x_ref, gate_ref, hf_ref, ab_ref, bb_ref, wout_ref, out_ref,
                    y_scr, hl_scr, ac_scr, sum_scr, carry_scr, *, tiles_per_seq, n_tiles):
    tile = n_tiles - 1 - pl.program_id(0)

    @pl.when(tile % tiles_per_seq == tiles_per_seq - 1)
    def _():
        carry_scr[...] = jnp.zeros_like(carry_scr)

    ends = _local_scan(ab_ref, bb_ref, hl_scr, ac_scr, reverse=True)
    for c in range(LRU_HEADS):
        lanes = slice(c * LANES, (c + 1) * LANES)
        sum_scr[c] = (hf_ref[c].astype(_F32)
                      + _finish_scan(ends, hl_scr, ac_scr, carry_scr, c, reverse=True))
        for j in range(SUBLANES):
            rows = slice(j * SEG, (j + 1) * SEG)
            h = _segment_rows(sum_scr, c, j)
            y_scr[rows, lanes] = (h * gate_ref[rows, lanes].astype(_F32)).astype(_BF16)
    out_ref[...] = x_ref[...] + jnp.dot(y_scr[...], wout_ref[...],
                                        preferred_element_type=_F32)


def _lru_out(x, gate, hf, ab, bb, w_out, seq_len):
    n = x.shape[0]
    nt = n // TOKEN_TILE
    t = TOKEN_TILE
    tok = pl.BlockSpec((t, D_MODEL), lambda i: (nt - 1 - i, 0))
    heads = pl.BlockSpec((LRU_HEADS, t, LANES), lambda i: (0, nt - 1 - i, 0))
    head_scr = pltpu.VMEM((LRU_HEADS, t, LANES), _F32)
    return pl.pallas_call(
        functools.partial(_lru_out_kernel, tiles_per_seq=seq_len // t, n_tiles=nt),
        grid=(nt,),
        in_specs=[tok, tok, heads, heads, heads, _const_spec((D_MODEL, D_MODEL))],
        out_specs=tok,
        out_shape=jax.ShapeDtypeStruct((n, D_MODEL), _F32),
        scratch_shapes=[
            pltpu.VMEM((t, D_MODEL), _BF16),
            head_scr, head_scr, head_scr,
            pltpu.VMEM((LRU_HEADS, LANES), _F32),
        ],
        compiler_params=pltpu.CompilerParams(
            dimension_semantics=("arbitrary",), vmem_limit_bytes=VMEM_LIMIT),
        name="lru_out",
    )(x, gate, hf, ab, bb, w_out)


def _ffn_kernel(x_ref, xp_ref, xn_ref, g_ref, wup_ref, cw_ref, cb_ref, wdn_ref, gfin_ref,
                out_ref, h_scr, acc_scr, *, tiles_per_seq, final_norm):
    tile = pl.program_id(0)
    t = TOKEN_TILE
    rows = t // CONV_STRIDE
    nblk = FF_CHUNK // LANES
    n_chunks = D_FF // FF_CHUNK
    xa = _halo_tile(x_ref, xp_ref, xn_ref, tiles_per_seq, tile)
    xan = _rms_norm(xa, g_ref[...]).astype(_BF16)

    def up_project(ck):
        for half in range(2):
            col0 = half * D_FF + ck * FF_CHUNK
            h = jnp.dot(xan, wup_ref[:, col0:col0 + FF_CHUNK], preferred_element_type=_F32)
            for b in range(nblk):
                _store_time_ordered(h_scr, (ck % 2) * 2 * nblk + half * nblk + b,
                                    h[:, b * LANES:(b + 1) * LANES])

    def gated_conv(ck):
        blocks = []
        for b in range(nblk):
            pieces = []
            for s in range(CONV_STRIDE):
                conv = []
                for half in range(2):
                    col = half * D_FF + ck * FF_CHUNK + b * LANES
                    cw = cw_ref[:, col:col + LANES].astype(_BF16)
                    val = cb_ref[:, col:col + LANES].astype(_BF16)
                    for k in range(3):
                        win = h_scr[(ck % 2) * 2 * nblk + half * nblk + b,
                                    pl.ds(HALO - 1 + s + k, rows, stride=CONV_STRIDE), :]
                        val = val + win.astype(_BF16) * cw[k:k + 1, :]
                    conv.append(val)
                pieces.append(_gelu_tanh(conv[0]) * conv[1])
            blocks.append(jnp.concatenate(pieces, axis=0))
        return jnp.concatenate(blocks, axis=1)

    acc = jnp.zeros((t, D_MODEL), _F32)
    up_project(0)
    for ck in range(n_chunks):
        if ck + 1 < n_chunks:
            up_project(ck + 1)
        acc = acc + jnp.dot(gated_conv(ck), wdn_ref[ck * FF_CHUNK:(ck + 1) * FF_CHUNK, :],
                            preferred_element_type=_F32)
    for c in range(D_MODEL // LANES):
        for s in range(CONV_STRIDE):
            acc_scr[c, pl.ds(s, rows, stride=CONV_STRIDE), :] = (
                acc[s * rows:(s + 1) * rows, c * LANES:(c + 1) * LANES])
    y = x_ref[...] + jnp.concatenate(
        [acc_scr[c] for c in range(D_MODEL // LANES)], axis=1)
    if final_norm:
        y = _rms_norm(y, gfin_ref[...])
    out_ref[...] = y


def _conv_ffn(x, g, w_up, conv_w, conv_b, w_down, g_final, seq_len, final_norm):
    n = x.shape[0]
    t = TOKEN_TILE
    return pl.pallas_call(
        functools.partial(_ffn_kernel, tiles_per_seq=seq_len // t, final_norm=final_norm),
        grid=(n // t,),
        in_specs=_halo_specs(n) + [
            _const_spec((1, D_MODEL)),
            _const_spec((D_MODEL, 2 * D_FF)),
            _const_spec((3, 2 * D_FF)),
            _const_spec((1, 2 * D_FF)),
            _const_spec((D_FF, D_MODEL)),
            _const_spec((1, D_MODEL)),
        ],
        out_specs=pl.BlockSpec((t, D_MODEL), lambda i: (i, 0)),
        out_shape=jax.ShapeDtypeStruct((n, D_MODEL), _F32),
        scratch_shapes=[
            pltpu.VMEM((4 * FF_CHUNK // LANES, t + 2 * HALO, LANES), _F32),
            pltpu.VMEM((D_MODEL // LANES, t, LANES), _F32),
        ],
        compiler_params=pltpu.CompilerParams(
            dimension_semantics=("arbitrary",), vmem_limit_bytes=VMEM_LIMIT),
        name="conv_ffn",
    )(x, x, x, g, w_up, conv_w, conv_b, w_down, g_final)


def _qkv_kernel(x_ref, g_ref, w_ref, q_ref, k_ref, v_ref):
    xn = _rms_norm(x_ref[...], g_ref[...]).astype(_BF16)
    qkv = jnp.dot(xn, w_ref[...], preferred_element_type=_F32)
    q_ref[...] = (qkv[:, :D_MODEL] * (HEAD_DIM ** -0.5 * LOG2_E)).astype(_BF16)
    k_ref[...] = qkv[:, D_MODEL:2 * D_MODEL].astype(_BF16)
    v_ref[...] = qkv[:, 2 * D_MODEL:].astype(_BF16)


def _qkv(x, g, w_qkv):
    n = x.shape[0]
    t = TOKEN_TILE
    tok = pl.BlockSpec((t, D_MODEL), lambda i: (i, 0))
    out = jax.ShapeDtypeStruct((n, D_MODEL), _BF16)
    return pl.pallas_call(
        _qkv_kernel,
        grid=(n // t,),
        in_specs=[tok, _const_spec((1, D_MODEL)), _const_spec((D_MODEL, 3 * D_MODEL))],
        out_specs=[tok, tok, tok],
        out_shape=[out, out, out],
        compiler_params=pltpu.CompilerParams(
            dimension_semantics=("arbitrary",), vmem_limit_bytes=VMEM_LIMIT),
        name="qkv",
    )(x, g, w_qkv)


def _attn_kernel(q_ref, k_ref, v_ref, bias_ref, x_ref, wo_ref, out_ref, s_scr, o_scr,
                 *, blocks_per_seq, n_tiles):
    win = WIN_ROWS * GRID_W
    n_pairs = ATTN_HEADS // 2
    rows_per_seq = blocks_per_seq * ROW_BLOCK
    tile = pl.program_id(0)
    blk = tile % blocks_per_seq
    win_row0 = (jnp.clip(tile - 1, 0, n_tiles - 3) - (tile - blk)) * ROW_BLOCK
    lane = lax.broadcasted_iota(jnp.int32, (GRID_W, LANES), 1)
    low = lane < HEAD_DIM
    ones = jnp.ones((win, LANES), _BF16)

    def row_body(qr, carry):
        r = blk * ROW_BLOCK + qr
        rs = jnp.clip(r - WIN_ROWS // 2, 0, rows_per_seq - WIN_ROWS)
        shift = (WIN_ROWS - 1) - (r - rs)
        par = shift % 2
        m0 = shift // 2
        kstart = pl.multiple_of((rs - win_row0) * GRID_W, GRID_W)
        qrow = qr * GRID_W
        row_max = []
        for hp in range(n_pairs):
            lanes = slice(hp * LANES, (hp + 1) * LANES)
            q2 = q_ref[pl.ds(qrow, GRID_W), lanes]
            zero = jnp.zeros_like(q2)
            qs = jnp.concatenate([jnp.where(low, q2, zero), jnp.where(low, zero, q2)], axis=0)
            s = lax.dot_general(qs, k_ref[pl.ds(kstart, win), lanes], (((1,), (1,)), ((), ())),
                                preferred_element_type=_F32)
            bias = jnp.concatenate(
                [jnp.concatenate([bias_ref[par, 2 * hp + hh, m0 + m] for m in range(4)], axis=1)
                 for hh in range(2)], axis=0)
            s = s + bias
            s_scr[(qr % 2) * n_pairs + hp] = s
            row_max.append(jnp.max(s, axis=-1, keepdims=True))
        for hp in range(n_pairs):
            lanes = slice(hp * LANES, (hp + 1) * LANES)
            p = jnp.exp2(s_scr[(qr % 2) * n_pairs + hp] - row_max[hp]).astype(_BF16)
            v_ones = jnp.concatenate([v_ref[pl.ds(kstart, win), lanes], ones], axis=1)
            ov = jnp.dot(p, v_ones, preferred_element_type=_F32)
            o = ov[:, :LANES] * (1.0 / ov[:, LANES:])
            o2 = jnp.where(low, o[:GRID_W], o[GRID_W:])
            o_scr[pl.ds(qrow, GRID_W), lanes] = o2.astype(_BF16)
        return carry

    for qr in range(ROW_BLOCK):
        row_body(qr, 0)
    out_ref[...] = x_ref[...] + jnp.dot(o_scr[...], wo_ref[...], preferred_element_type=_F32)


def _attention(x, q, k, v, bias_tab, w_o, seq_len):
    n = x.shape[0]
    t = TOKEN_TILE
    nt = n // t
    tok = pl.BlockSpec((t, D_MODEL), lambda i: (i, 0))
    window = pl.BlockSpec((pl.Element(3 * t), pl.Element(D_MODEL)),
                          lambda i: (jnp.clip(i - 1, 0, nt - 3) * t, 0))
    return pl.pallas_call(
        functools.partial(_attn_kernel, blocks_per_seq=seq_len // t, n_tiles=nt),
        grid=(nt,),
        in_specs=[tok, window, window, _const_spec(bias_tab.shape), tok,
                  _const_spec((D_MODEL, D_MODEL))],
        out_specs=tok,
        out_shape=jax.ShapeDtypeStruct((n, D_MODEL), _F32),
        scratch_shapes=[
            pltpu.VMEM((ATTN_HEADS, 2 * GRID_W, WIN_ROWS * GRID_W), _F32),
            pltpu.VMEM((t, D_MODEL), _BF16),
        ],
        compiler_params=pltpu.CompilerParams(
            dimension_semantics=("arbitrary",), vmem_limit_bytes=VMEM_LIMIT),
        name="attention",
    )(q, k, v, bias_tab, x, w_o)


def _bias_table(rpb):
    col = jnp.arange(GRID_W, dtype=jnp.int32)
    col_start = jnp.clip(col - WIN_COLS // 2, 0, GRID_W - WIN_COLS)
    kc = jnp.arange(GRID_W, dtype=jnp.int32)
    valid = (kc[None, :] >= col_start[:, None]) & (kc[None, :] < col_start[:, None] + WIN_COLS)
    pad = GRID_W - WIN_COLS
    padded = jnp.pad(rpb.astype(_F32) * LOG2_E, ((0, 0), (0, 0), (pad, pad)))
    full = jnp.stack([padded[:, :, GRID_W - 1 - c:2 * GRID_W - 1 - c] for c in range(GRID_W)],
                     axis=2)
    full = jnp.where(valid[None, None], full, NEG_BIG)
    flat = jnp.transpose(full, (0, 2, 1, 3)).reshape(ATTN_HEADS, GRID_W, -1)
    flat = jnp.pad(flat, ((0, 0), (0, 0), (0, 64)), constant_values=NEG_BIG)
    tabs = []
    for par in range(2):
        sl = flat[:, :, 64 * par:64 * par + 7 * LANES]
        tabs.append(jnp.transpose(sl.reshape(ATTN_HEADS, GRID_W, 7, LANES), (0, 2, 1, 3)))
    return jnp.stack(tabs, axis=0)


def _trunk(x, seq_len, p):
    gate, rec = _lru_proj(x, p["norm_mix"][0], p["w_in"])
    hf, ab, bb = _lru_in(rec, p["lru_conv_w"], p["lru_conv_b"], p["gate_w"], p["lam"], seq_len)
    x = _lru_out(x, gate, hf, ab, bb, p["w_out"], seq_len)
    x = _conv_ffn(x, p["norm_ffn"][0], p["w_up"][0], p["ffn_conv_w"][0], p["ffn_conv_b"][0],
                  p["w_down"][0], p["norm_final"], seq_len, final_norm=False)
    q, k, v = _qkv(x, p["norm_mix"][1], p["w_qkv"])
    x = _attention(x, q, k, v, p["bias_tab"], p["w_o"], seq_len)
    x = _conv_ffn(x, p["norm_ffn"][1], p["w_up"][1], p["ffn_conv_w"][1], p["ffn_conv_b"][1],
                  p["w_down"][1], p["norm_final"], seq_len, final_norm=True)
    return x


def kernel(x_prompt, x_sample, norm_mix, norm_ffn, norm_final, lru_w_in, lru_conv_w, lru_conv_b, lru_gate_w, lru_gate_b, lru_lambda, lru_w_out, attn_w_qkv, attn_rpb, attn_w_o, ffn_w_up, ffn_conv_w, ffn_conv_b, ffn_w_down):
    gw = jnp.transpose(lru_gate_w[0], (2, 3, 0, 1, 4)).reshape(
        LRU_HEADS, LRU_BLOCK, 4 * LRU_BLOCK).astype(_BF16)
    gb = (0.5 * jnp.transpose(lru_gate_b[0], (2, 0, 1, 3))).reshape(
        LRU_HEADS, 1, 4 * LRU_BLOCK)
    gb_hi = gb.astype(_BF16)
    gb_lo = (gb - gb_hi.astype(_F32)).astype(_BF16)
    gw = jnp.concatenate(
        [gw, gb_hi, gb_lo, jnp.zeros((LRU_HEADS, LRU_BLOCK - 2, 4 * LRU_BLOCK), _BF16)], axis=1)
    p = {
        "norm_mix": [norm_mix[i][None, :] for i in range(2)],
        "norm_ffn": [norm_ffn[i][None, :] for i in range(2)],
        "norm_final": norm_final[None, :],
        "w_in": lru_w_in[0].astype(_BF16),
        "lru_conv_w": 0.5 * lru_conv_w[0],
        "lru_conv_b": 0.5 * lru_conv_b[0][None, :],
        "gate_w": gw,
        "lam": lru_lambda[0],
        "w_out": lru_w_out[0].astype(_BF16),
        "w_qkv": attn_w_qkv[0].astype(_BF16),
        "bias_tab": _bias_table(attn_rpb[0]),
        "w_o": attn_w_o[0].astype(_BF16),
        "w_up": [ffn_w_up[i].astype(_BF16) for i in range(2)],
        "ffn_conv_w": [ffn_conv_w[i] for i in range(2)],
        "ffn_conv_b": [ffn_conv_b[i][None, :] for i in range(2)],
        "w_down": [ffn_w_down[i].astype(_BF16) for i in range(2)],
    }
    outs = []
    for x in (x_prompt, x_sample):
        b, seq_len, d = x.shape
        y = _trunk(x.reshape(b * seq_len, d), seq_len, p)
        outs.append(y.reshape(b, seq_len, d))
    return tuple(outs)
```

```python
import functools

import jax
import jax.numpy as jnp
from jax import lax
from jax.experimental import pallas as pl
from jax.experimental.pallas import tpu as pltpu

D_MODEL = 1024
LRU_HEADS = 8
LRU_BLOCK = 128
LRU_CONV = 4
LRU_C = 8.0
ATTN_HEADS = 16
HEAD_DIM = 64
GRID_W = 64
WIN_ROWS = 8
WIN_COLS = 16
D_FF = 2816
NORM_EPS = 1e-6

LANES = 128
SUBLANES = 8
TOKEN_TILE = 512
HALO = SUBLANES
SEG = TOKEN_TILE // SUBLANES
FF_CHUNK = 256
CONV_STRIDE = 4
ROW_BLOCK = TOKEN_TILE // GRID_W
VMEM_LIMIT = 56 * 1024 * 1024
NEG_BIG = -1e30
TINY = 1e-30
LOG2_E = 1.4426950408889634

_BF16 = jnp.bfloat16
_F32 = jnp.float32


def _const_spec(shape):
    nd = len(shape)
    return pl.BlockSpec(shape, lambda i: (0,) * nd, pipeline_mode=pl.Buffered(1))


def _rms_norm(x, g):
    ms = jnp.mean(x * x, axis=-1, keepdims=True)
    return x * lax.rsqrt(ms + NORM_EPS) * g


def _gelu_tanh(x):
    k0 = -2.0 * LOG2_E * 0.7978845608028654
    return x / (1.0 + jnp.exp2(x * ((k0 * 0.044715) * (x * x) + k0)))


def _softplus(z):
    e = jnp.exp(-jnp.abs(z))
    u = 1.0 + e
    log1p = jnp.where(u == 1.0, e, jnp.log(u) * (e / (u - 1.0)))
    return jnp.maximum(z, 0.0) + log1p


def _halo_tile(x_ref, xp_ref, xn_ref, tiles_per_seq, tile):
    t_in_seq = tile % tiles_per_seq
    keep_prev = jnp.where(t_in_seq == 0, 0.0, 1.0)
    keep_next = jnp.where(t_in_seq == tiles_per_seq - 1, 0.0, 1.0)
    return jnp.concatenate(
        [x_ref[...], xp_ref[...] * keep_prev, xn_ref[...] * keep_next], axis=0)


def _store_time_ordered(dst_ref, slot, val):
    t = TOKEN_TILE
    dst_ref[slot, HALO:HALO + t, :] = val[:t]
    dst_ref[slot, 0:HALO, :] = val[t:t + HALO]
    dst_ref[slot, HALO + t:, :] = val[t + HALO:]


def _halo_specs(n_tokens):
    per = TOKEN_TILE // HALO
    last = n_tokens // HALO - 1
    return [
        pl.BlockSpec((TOKEN_TILE, D_MODEL), lambda i: (i, 0)),
        pl.BlockSpec((HALO, D_MODEL), lambda i: (jnp.maximum(i * per - 1, 0), 0)),
        pl.BlockSpec((HALO, D_MODEL), lambda i: (jnp.minimum((i + 1) * per, last), 0)),
    ]


def _to_scan_order(dst_ref, c, val):
    for j in range(SUBLANES):
        dst_ref[c, pl.ds(j, SEG, stride=SUBLANES), :] = val[j * SEG:(j + 1) * SEG, :]


def _segment_rows(src_ref, c, j):
    return src_ref[c, pl.ds(j, SEG, stride=SUBLANES), :]


def _local_scan(sa_ref, sb_ref, hl_ref, ac_ref, reverse):
    def body(k, carry):
        s = (SEG - 1 - k) if reverse else k
        row = pl.multiple_of(s * SUBLANES, SUBLANES)
        out = []
        for c in range(LRU_HEADS):
            h, acum = carry[c]
            a = sa_ref[c, pl.ds(row, SUBLANES), :]
            b = sb_ref[c, pl.ds(row, SUBLANES), :]
            h = a * h + b
            acum = a * acum
            hl_ref[c, pl.ds(row, SUBLANES), :] = h
            ac_ref[c, pl.ds(row, SUBLANES), :] = acum
            out.append((h, acum))
        return tuple(out)

    init = tuple((jnp.zeros((SUBLANES, LANES), _F32), jnp.ones((SUBLANES, LANES), _F32))
                 for _ in range(LRU_HEADS))
    return lax.fori_loop(0, SEG, body, init, unroll=8)


def _chain_segments(h_end, a_end, carry_in, reverse):
    order = range(SUBLANES - 1, -1, -1) if reverse else range(SUBLANES)
    states = [None] * SUBLANES
    cur = carry_in
    for j in order:
        states[j] = cur
        cur = h_end[j:j + 1, :] + a_end[j:j + 1, :] * cur
    return states, cur


def _finish_scan(ends, hl_ref, ac_ref, carry_scr, c, reverse):
    states, carry = _chain_segments(ends[c][0], ends[c][1], carry_scr[c:c + 1, :], reverse)
    carry_scr[c:c + 1, :] = carry
    seg_state = jnp.concatenate(states, axis=0)[None]
    shape3 = (SEG, SUBLANES, LANES)
    fixed = hl_ref[c].reshape(shape3) + ac_ref[c].reshape(shape3) * seg_state
    return fixed.reshape(TOKEN_TILE, LANES)


def _lru_in_kernel(x_ref, xp_ref, xn_ref, g_ref, win_ref, cw_ref, cb_ref, gw_ref,
                   lam_ref, gate_out, hf_out, ab_out, bb_out,
                   rec_scr, xc_scr, sa_scr, sb_scr, hl_scr, ac_scr, carry_scr,
                   *, tiles_per_seq):
    tile = pl.program_id(0)
    t = TOKEN_TILE
    rows = t // CONV_STRIDE
    xa = _halo_tile(x_ref, xp_ref, xn_ref, tiles_per_seq, tile)
    xan = _rms_norm(xa, g_ref[...]).astype(_BF16)
    proj = jnp.dot(xan, win_ref[...], preferred_element_type=_F32)
    gate_out[...] = _gelu_tanh(proj[:t, :D_MODEL].astype(_BF16))
    for c in range(LRU_HEADS):
        _store_time_ordered(rec_scr, c, proj[:, D_MODEL + c * LANES:D_MODEL + (c + 1) * LANES])

    @pl.when(tile % tiles_per_seq == 0)
    def _():
        carry_scr[...] = jnp.zeros_like(carry_scr)

    half_decay = (-0.5 * LRU_C * LOG2_E) * _softplus(-lam_ref[...])

    for c in range(LRU_HEADS):
        lanes = slice(c * LANES, (c + 1) * LANES)
        cw = cw_ref[:, lanes]
        cb = cb_ref[:, lanes]
        for s in range(CONV_STRIDE):
            acc = cb
            for k in range(LRU_CONV):
                win = rec_scr[c, pl.ds(HALO - 2 + s + k, rows, stride=CONV_STRIDE), :]
                acc = acc + win * cw[k:k + 1, :]
            xc_scr[c % 2, pl.ds(s, rows, stride=CONV_STRIDE), :] = acc
        half_xc = xc_scr[c % 2]
        lhs = jnp.concatenate([half_xc.astype(_BF16), jnp.ones((t, LANES), _BF16)], axis=1)
        th = jnp.tanh(jnp.dot(lhs, gw_ref[c], preferred_element_type=_F32))
        for d in range(2):
            t_r = th[:, (2 * d) * LANES:(2 * d + 1) * LANES]
            t_i = th[:, (2 * d + 1) * LANES:(2 * d + 2) * LANES]
            hd = half_decay[d:d + 1, lanes]
            a = jnp.exp2(t_r * hd + hd)
            y = 1.0 - a * a
            mult = y * lax.rsqrt(jnp.maximum(y, TINY))
            b = (t_i + 1.0) * (mult * half_xc)
            if d == 0:
                _to_scan_order(sa_scr, c, a)
                _to_scan_order(sb_scr, c, b)
            else:
                _to_scan_order(ab_out, c, a)
                _to_scan_order(bb_out, c, b)

    ends = _local_scan(sa_scr, sb_scr, hl_scr, ac_scr, reverse=False)
    for c in range(LRU_HEADS):
        hf_out[c] = _finish_scan(ends, hl_scr, ac_scr, carry_scr, c, reverse=False).astype(_BF16)


def _lru_in(x, g, w_in, conv_w, conv_b, gate_w, lam, seq_len):
    n = x.shape[0]
    nt = n // TOKEN_TILE
    t = TOKEN_TILE
    tok = pl.BlockSpec((t, D_MODEL), lambda i: (i, 0))
    gate_shape = jax.ShapeDtypeStruct((n, D_MODEL), _BF16)
    hf_shape = jax.ShapeDtypeStruct((LRU_HEADS, n, LANES), _BF16)
    ab_shape = jax.ShapeDtypeStruct((LRU_HEADS, n, LANES), _F32)
    heads = pl.BlockSpec((LRU_HEADS, t, LANES), lambda i: (0, i, 0))
    head_scr = pltpu.VMEM((LRU_HEADS, t, LANES), _F32)
    return pl.pallas_call(
        functools.partial(_lru_in_kernel, tiles_per_seq=seq_len // t),
        grid=(nt,),
        in_specs=_halo_specs(n) + [
            _const_spec((1, D_MODEL)),
            _const_spec((D_MODEL, 2 * D_MODEL)),
            _const_spec((LRU_CONV, D_MODEL)),
            _const_spec((1, D_MODEL)),
            _const_spec((LRU_HEADS, 2 * LRU_BLOCK, 4 * LRU_BLOCK)),
            _const_spec((2, D_MODEL)),
        ],
        out_specs=[tok, heads, heads, heads],
        out_shape=[gate_shape, hf_shape, ab_shape, ab_shape],
        scratch_shapes=[
            pltpu.VMEM((LRU_HEADS, t + 2 * HALO, LANES), _F32),
            pltpu.VMEM((2, t, LANES), _F32),
            head_scr, head_scr, head_scr, head_scr,
            pltpu.VMEM((LRU_HEADS, LANES), _F32),
        ],
        compiler_params=pltpu.CompilerParams(
            dimension_semantics=("arbitrary",), vmem_limit_bytes=VMEM_LIMIT),
        name="lru_in",
    )(x, x, x, g, w_in, conv_w, conv_b, gate_w, lam)


def _lru_out_kernel(x_ref, gate_ref, hf_ref, ab_ref, bb_ref, wout_ref, out_ref,
                    y_scr, hl_scr, ac_scr, sum_scr, carry_scr, *, tiles_per_seq, n_tiles):
    tile = n_tiles - 1 - pl.program_id(0)

    @pl.when(tile % tiles_per_seq == tiles_per_seq - 1)
    def _():
        carry_scr[...] = jnp.zeros_like(carry_scr)

    ends = _local_scan(ab_ref, bb_ref, hl_scr, ac_scr, reverse=True)
    for c in range(LRU_HEADS):
        lanes = slice(c * LANES, (c + 1) * LANES)
        sum_scr[c] = (hf_ref[c].astype(_F32)
                      + _finish_scan(ends, hl_scr, ac_scr, carry_scr, c, reverse=True))
        for j in range(SUBLANES):
            rows = slice(j * SEG, (j + 1) * SEG)
            h = _segment_rows(sum_scr, c, j)
            y_scr[rows, lanes] = (h * gate_ref[rows, lanes].astype(_F32)).astype(_BF16)
    out_ref[...] = x_ref[...] + jnp.dot(y_scr[...], wout_ref[...],
                                        preferred_element_type=_F32)


def _lru_out(x, gate, hf, ab, bb, w_out, seq_len):
    n = x.shape[0]
    nt = n // TOKEN_TILE
    t = TOKEN_TILE
    tok = pl.BlockSpec((t, D_MODEL), lambda i: (nt - 1 - i, 0))
    heads = pl.BlockSpec((LRU_HEADS, t, LANES), lambda i: (0, nt - 1 - i, 0))
    head_scr = pltpu.VMEM((LRU_HEADS, t, LANES), _F32)
    return pl.pallas_call(
        functools.partial(_lru_out_kernel, tiles_per_seq=seq_len // t, n_tiles=nt),
        grid=(nt,),
        in_specs=[tok, tok, heads, heads, heads, _const_spec((D_MODEL, D_MODEL))],
        out_specs=tok,
        out_shape=jax.ShapeDtypeStruct((n, D_MODEL), _F32),
        scratch_shapes=[
            pltpu.VMEM((t, D_MODEL), _BF16),
            head_scr, head_scr, head_scr,
            pltpu.VMEM((LRU_HEADS, LANES), _F32),
        ],
        compiler_params=pltpu.CompilerParams(
            dimension_semantics=("arbitrary",), vmem_limit_bytes=VMEM_LIMIT),
        name="lru_out",
    )(x, gate, hf, ab, bb, w_out)


def _ffn_kernel(x_ref, xp_ref, xn_ref, g_ref, wup_ref, cw_ref, cb_ref, wdn_ref, gfin_ref,
                out_ref, h_scr, acc_scr, *, tiles_per_seq, final_norm):
    tile = pl.program_id(0)
    t = TOKEN_TILE
    rows = t // CONV_STRIDE
    nblk = FF_CHUNK // LANES
    n_chunks = D_FF // FF_CHUNK
    xa = _halo_tile(x_ref, xp_ref, xn_ref, tiles_per_seq, tile)
    xan = _rms_norm(xa, g_ref[...]).astype(_BF16)

    def up_project(ck):
        for half in range(2):
            col0 = half * D_FF + ck * FF_CHUNK
            h = jnp.dot(xan, wup_ref[:, col0:col0 + FF_CHUNK], preferred_element_type=_F32)
            for b in range(nblk):
                _store_time_ordered(h_scr, (ck % 2) * 2 * nblk + half * nblk + b,
                                    h[:, b * LANES:(b + 1) * LANES])

    def gated_conv(ck):
        blocks = []
        for b in range(nblk):
            pieces = []
            for s in range(CONV_STRIDE):
                conv = []
                for half in range(2):
                    col = half * D_FF + ck * FF_CHUNK + b * LANES
                    cw = cw_ref[:, col:col + LANES].astype(_BF16)
                    val = cb_ref[:, col:col + LANES].astype(_BF16)
                    for k in range(3):
                        win = h_scr[(ck % 2) * 2 * nblk + half * nblk + b,
                                    pl.ds(HALO - 1 + s + k, rows, stride=CONV_STRIDE), :]
                        val = val + win.astype(_BF16) * cw[k:k + 1, :]
                    conv.append(val)
                pieces.append(_gelu_tanh(conv[0]) * conv[1])
            blocks.append(jnp.concatenate(pieces, axis=0))
        return jnp.concatenate(blocks, axis=1)

    acc = jnp.zeros((t, D_MODEL), _F32)
    up_project(0)
    for ck in range(n_chunks):
        if ck + 1 < n_chunks:
            up_project(ck + 1)
        acc = acc + jnp.dot(gated_conv(ck), wdn_ref[ck * FF_CHUNK:(ck + 1) * FF_CHUNK, :],
                            preferred_element_type=_F32)
    for c in range(D_MODEL // LANES):
        for s in range(CONV_STRIDE):
            acc_scr[c, pl.ds(s, rows, stride=CONV_STRIDE), :] = (
                acc[s * rows:(s + 1) * rows, c * LANES:(c + 1) * LANES])
    y = x_ref[...] + jnp.concatenate(
        [acc_scr[c] for c in range(D_MODEL // LANES)], axis=1)
    if final_norm:
        y = _rms_norm(y, gfin_ref[...])
    out_ref[...] = y


def _conv_ffn(x, g, w_up, conv_w, conv_b, w_down, g_final, seq_len, final_norm):
    n = x.shape[0]
    t = TOKEN_TILE
    return pl.pallas_call(
        functools.partial(_ffn_kernel, tiles_per_seq=seq_len // t, final_norm=final_norm),
        grid=(n // t,),
        in_specs=_halo_specs(n) + [
            _const_spec((1, D_MODEL)),
            _const_spec((D_MODEL, 2 * D_FF)),
            _const_spec((3, 2 * D_FF)),
            _const_spec((1, 2 * D_FF)),
            _const_spec((D_FF, D_MODEL)),
            _const_spec((1, D_MODEL)),
        ],
        out_specs=pl.BlockSpec((t, D_MODEL), lambda i: (i, 0)),
        out_shape=jax.ShapeDtypeStruct((n, D_MODEL), _F32),
        scratch_shapes=[
            pltpu.VMEM((4 * FF_CHUNK // LANES, t + 2 * HALO, LANES), _F32),
            pltpu.VMEM((D_MODEL // LANES, t, LANES), _F32),
        ],
        compiler_params=pltpu.CompilerParams(
            dimension_semantics=("arbitrary",), vmem_limit_bytes=VMEM_LIMIT),
        name="conv_ffn",
    )(x, x, x, g, w_up, conv_w, conv_b, w_down, g_final)


def _qkv_kernel(x_ref, g_ref, w_ref, q_ref, k_ref, v_ref):
    xn = _rms_norm(x_ref[...], g_ref[...]).astype(_BF16)
    qkv = jnp.dot(xn, w_ref[...], preferred_element_type=_F32)
    q_ref[...] = (qkv[:, :D_MODEL] * (HEAD_DIM ** -0.5 * LOG2_E)).astype(_BF16)
    k_ref[...] = qkv[:, D_MODEL:2 * D_MODEL].astype(_BF16)
    v_ref[...] = qkv[:, 2 * D_MODEL:].astype(_BF16)


def _qkv(x, g, w_qkv):
    n = x.shape[0]
    t = TOKEN_TILE
    tok = pl.BlockSpec((t, D_MODEL), lambda i: (i, 0))
    out = jax.ShapeDtypeStruct((n, D_MODEL), _BF16)
    return pl.pallas_call(
        _qkv_kernel,
        grid=(n // t,),
        in_specs=[tok, _const_spec((1, D_MODEL)), _const_spec((D_MODEL, 3 * D_MODEL))],
        out_specs=[tok, tok, tok],
        out_shape=[out, out, out],
        compiler_params=pltpu.CompilerParams(
            dimension_semantics=("arbitrary",), vmem_limit_bytes=VMEM_LIMIT),
        name="qkv",
    )(x, g, w_qkv)


def _attn_kernel(q_ref, k_ref, v_ref, bias_ref, x_ref, wo_ref, out_ref, s_scr, o_scr,
                 *, blocks_per_seq, n_tiles):
    win = WIN_ROWS * GRID_W
    n_pairs = ATTN_HEADS // 2
    rows_per_seq = blocks_per_seq * ROW_BLOCK
    tile = pl.program_id(0)
    blk = tile % blocks_per_seq
    win_row0 = (jnp.clip(tile - 1, 0, n_tiles - 3) - (tile - blk)) * ROW_BLOCK
    lane = lax.broadcasted_iota(jnp.int32, (GRID_W, LANES), 1)
    low = lane < HEAD_DIM
    ones = jnp.ones((win, LANES), _BF16)

    def row_body(qr, carry):
        r = blk * ROW_BLOCK + qr
        rs = jnp.clip(r - WIN_ROWS // 2, 0, rows_per_seq - WIN_ROWS)
        shift = (WIN_ROWS - 1) - (r - rs)
        par = shift % 2
        m0 = shift // 2
        kstart = pl.multiple_of((rs - win_row0) * GRID_W, GRID_W)
        qrow = qr * GRID_W
        row_max = []
        for hp in range(n_pairs):
            lanes = slice(hp * LANES, (hp + 1) * LANES)
            q2 = q_ref[pl.ds(qrow, GRID_W), lanes]
            zero = jnp.zeros_like(q2)
            qs = jnp.concatenate([jnp.where(low, q2, zero), jnp.where(low, zero, q2)], axis=0)
            s = lax.dot_general(qs, k_ref[pl.ds(kstart, win), lanes], (((1,), (1,)), ((), ())),
                                preferred_element_type=_F32)
            bias = jnp.concatenate(
                [jnp.concatenate([bias_ref[par, 2 * hp + hh, m0 + m] for m in range(4)], axis=1)
                 for hh in range(2)], axis=0)
            s = s + bias
            s_scr[(qr % 2) * n_pairs + hp] = s
            row_max.append(jnp.max(s, axis=-1, keepdims=True))
        for hp in range(n_pairs):
            lanes = slice(hp * LANES, (hp + 1) * LANES)
            p = jnp.exp2(s_scr[(qr % 2) * n_pairs + hp] - row_max[hp]).astype(_BF16)
            v_ones = jnp.concatenate([v_ref[pl.ds(kstart, win), lanes], ones], axis=1)
            ov = jnp.dot(p, v_ones, preferred_element_type=_F32)
            o = ov[:, :LANES] * (1.0 / ov[:, LANES:])
            o2 = jnp.where(low, o[:GRID_W], o[GRID_W:])
            o_scr[pl.ds(qrow, GRID_W), lanes] = o2.astype(_BF16)
        return carry

    for qr in range(ROW_BLOCK):
        row_body(qr, 0)
    out_ref[...] = x_ref[...] + jnp.dot(o_scr[...], wo_ref[...], preferred_element_type=_F32)


def _attention(x, q, k, v, bias_tab, w_o, seq_len):
    n = x.shape[0]
    t = TOKEN_TILE
    nt = n // t
    tok = pl.BlockSpec((t, D_MODEL), lambda i: (i, 0))
    window = pl.BlockSpec((pl.Element(3 * t), pl.Element(D_MODEL)),
                          lambda i: (jnp.clip(i - 1, 0, nt - 3) * t, 0))
    return pl.pallas_call(
        functools.partial(_attn_kernel, blocks_per_seq=seq_len // t, n_tiles=nt),
        grid=(nt,),
        in_specs=[tok, window, window, _const_spec(bias_tab.shape), tok,
                  _const_spec((D_MODEL, D_MODEL))],
        out_specs=tok,
        out_shape=jax.ShapeDtypeStruct((n, D_MODEL), _F32),
        scratch_shapes=[
            pltpu.VMEM((ATTN_HEADS, 2 * GRID_W, WIN_ROWS * GRID_W), _F32),
            pltpu.VMEM((t, D_MODEL), _BF16),
        ],
        compiler_params=pltpu.CompilerParams(
            dimension_semantics=("arbitrary",), vmem_limit_bytes=VMEM_LIMIT),
        name="attention",
    )(q, k, v, bias_tab, x, w_o)


def _bias_table(rpb):
    col = jnp.arange(GRID_W, dtype=jnp.int32)
    col_start = jnp.clip(col - WIN_COLS // 2, 0, GRID_W - WIN_COLS)
    kc = jnp.arange(GRID_W, dtype=jnp.int32)
    valid = (kc[None, :] >= col_start[:, None]) & (kc[None, :] < col_start[:, None] + WIN_COLS)
    pad = GRID_W - WIN_COLS
    padded = jnp.pad(rpb.astype(_F32) * LOG2_E, ((0, 0), (0, 0), (pad, pad)))
    full = jnp.stack([padded[:, :, GRID_W - 1 - c:2 * GRID_W - 1 - c] for c in range(GRID_W)],
                     axis=2)
    full = jnp.where(valid[None, None], full, NEG_BIG)
    flat = jnp.transpose(full, (0, 2, 1, 3)).reshape(ATTN_HEADS, GRID_W, -1)
    flat = jnp.pad(flat, ((0, 0), (0, 0), (0, 64)), constant_values=NEG_BIG)
    tabs = []
    for par in range(2):
        sl = flat[:, :, 64 * par:64 * par + 7 * LANES]
        tabs.append(jnp.transpose(sl.reshape(ATTN_HEADS, GRID_W, 7, LANES), (0, 2, 1, 3)))
    return jnp.stack(tabs, axis=0)


def _trunk(x, seq_len, p):
    gate, hf, ab, bb = _lru_in(x, p["norm_mix"][0], p["w_in"], p["lru_conv_w"], p["lru_conv_b"],
                               p["gate_w"], p["lam"], seq_len)
    x = _lru_out(x, gate, hf, ab, bb, p["w_out"], seq_len)
    x = _conv_ffn(x, p["norm_ffn"][0], p["w_up"][0], p["ffn_conv_w"][0], p["ffn_conv_b"][0],
                  p["w_down"][0], p["norm_final"], seq_len, final_norm=False)
    q, k, v = _qkv(x, p["norm_mix"][1], p["w_qkv"])
    x = _attention(x, q, k, v, p["bias_tab"], p["w_o"], seq_len)
    x = _conv_ffn(x, p["norm_ffn"][1], p["w_up"][1], p["ffn_conv_w"][1], p["ffn_conv_b"][1],
                  p["w_down"][1], p["norm_final"], seq_len, final_norm=True)
    return x


def kernel(x_prompt, x_sample, norm_mix, norm_ffn, norm_final, lru_w_in, lru_conv_w, lru_conv_b, lru_gate_w, lru_gate_b, lru_lambda, lru_w_out, attn_w_qkv, attn_rpb, attn_w_o, ffn_w_up, ffn_conv_w, ffn_conv_b, ffn_w_down):
    gw = jnp.transpose(lru_gate_w[0], (2, 3, 0, 1, 4)).reshape(
        LRU_HEADS, LRU_BLOCK, 4 * LRU_BLOCK).astype(_BF16)
    gb = (0.5 * jnp.transpose(lru_gate_b[0], (2, 0, 1, 3))).reshape(
        LRU_HEADS, 1, 4 * LRU_BLOCK)
    gb_hi = gb.astype(_BF16)
    gb_lo = (gb - gb_hi.astype(_F32)).astype(_BF16)
    gw = jnp.concatenate(
        [gw, gb_hi, gb_lo, jnp.zeros((LRU_HEADS, LRU_BLOCK - 2, 4 * LRU_BLOCK), _BF16)], axis=1)
    p = {
        "norm_mix": [norm_mix[i][None, :] for i in range(2)],
        "norm_ffn": [norm_ffn[i][None, :] for i in range(2)],
        "norm_final": norm_final[None, :],
        "w_in": lru_w_in[0].astype(_BF16),
        "lru_conv_w": 0.5 * lru_conv_w[0],
        "lru_conv_b": 0.5 * lru_conv_b[0][None, :],
        "gate_w": gw,
        "lam": lru_lambda[0],
        "w_out": lru_w_out[0].astype(_BF16),
        "w_qkv": attn_w_qkv[0].astype(_BF16),
        "bias_tab": _bias_table(attn_rpb[0]),
        "w_o": attn_w_o[0].astype(_BF16),
        "w_up": [ffn_w_up[i].astype(_BF16) for i in range(2)],
        "ffn_conv_w": [ffn_conv_w[i] for i in range(2)],
        "ffn_conv_b": [ffn_conv_b[i][None, :] for i in range(2)],
        "w_down": [ffn_w_down[i].astype(_BF16) for i in range(2)],
    }
    outs = []
    for x in (x_prompt, x_sample):
        b, seq_len, d = x.shape
        y = _trunk(x.reshape(b * seq_len, d), seq_len, p)
        outs.append(y.reshape(b, seq_len, d))
    return tuple(outs)
```

```python
import functools

import jax
import jax.numpy as jnp
from jax import lax
from jax.experimental import pallas as pl
from jax.experimental.pallas import tpu as pltpu

D_MODEL = 1024
LRU_HEADS = 8
LRU_BLOCK = 128
LRU_CONV = 4
LRU_C = 8.0
ATTN_HEADS = 16
HEAD_DIM = 64
GRID_W = 64
WIN_ROWS = 8
WIN_COLS = 16
D_FF = 2816
NORM_EPS = 1e-6

LANES = 128
SUBLANES = 8
TOKEN_TILE = 512
HALO = SUBLANES
SEG = TOKEN_TILE // SUBLANES
FF_CHUNK = 256
CONV_STRIDE = 4
ROW_BLOCK = TOKEN_TILE // GRID_W
VMEM_LIMIT = 56 * 1024 * 1024
NEG_BIG = -1e30
TINY = 1e-30
LOG2_E = 1.4426950408889634

_BF16 = jnp.bfloat16
_F32 = jnp.float32


def _const_spec(shape):
    nd = len(shape)
    return pl.BlockSpec(shape, lambda i: (0,) * nd, pipeline_mode=pl.Buffered(1))


def _rms_norm(x, g):
    ms = jnp.mean(x * x, axis=-1, keepdims=True)
    return x * lax.rsqrt(ms + NORM_EPS) * g


def _gelu_tanh(x):
    k0 = -2.0 * LOG2_E * 0.7978845608028654
    return x / (1.0 + jnp.exp2(x * ((k0 * 0.044715) * (x * x) + k0)))


def _softplus(z):
    e = jnp.exp(-jnp.abs(z))
    u = 1.0 + e
    log1p = jnp.where(u == 1.0, e, jnp.log(u) * (e / (u - 1.0)))
    return jnp.maximum(z, 0.0) + log1p


def _halo_tile(x_ref, xp_ref, xn_ref, tiles_per_seq, tile):
    t_in_seq = tile % tiles_per_seq
    keep_prev = jnp.where(t_in_seq == 0, 0.0, 1.0)
    keep_next = jnp.where(t_in_seq == tiles_per_seq - 1, 0.0, 1.0)
    return jnp.concatenate(
        [x_ref[...], xp_ref[...] * keep_prev, xn_ref[...] * keep_next], axis=0)


def _store_time_ordered(dst_ref, slot, val):
    t = TOKEN_TILE
    dst_ref[slot, HALO:HALO + t, :] = val[:t]
    dst_ref[slot, 0:HALO, :] = val[t:t + HALO]
    dst_ref[slot, HALO + t:, :] = val[t + HALO:]


def _halo_specs(n_tokens):
    per = TOKEN_TILE // HALO
    last = n_tokens // HALO - 1
    return [
        pl.BlockSpec((TOKEN_TILE, D_MODEL), lambda i: (i, 0)),
        pl.BlockSpec((HALO, D_MODEL), lambda i: (jnp.maximum(i * per - 1, 0), 0)),
        pl.BlockSpec((HALO, D_MODEL), lambda i: (jnp.minimum((i + 1) * per, last), 0)),
    ]


def _to_scan_order(dst_ref, c, val):
    for j in range(SUBLANES):
        dst_ref[c, pl.ds(j, SEG, stride=SUBLANES), :] = val[j * SEG:(j + 1) * SEG, :]


def _segment_rows(src_ref, c, j):
    return src_ref[c, pl.ds(j, SEG, stride=SUBLANES), :]


def _local_scan(sa_ref, sb_ref, hl_ref, ac_ref, reverse):
    ends = []
    for c in range(LRU_HEADS):
        h = jnp.zeros((SUBLANES, LANES), _F32)
        acum = jnp.ones((SUBLANES, LANES), _F32)
        for k in range(SEG):
            row = ((SEG - 1 - k) if reverse else k) * SUBLANES
            a = sa_ref[c, row:row + SUBLANES, :]
            b = sb_ref[c, row:row + SUBLANES, :]
            h = a * h + b
            acum = a * acum
            hl_ref[c, row:row + SUBLANES, :] = h
            ac_ref[c, row:row + SUBLANES, :] = acum
        ends.append((h, acum))
    return tuple(ends)


def _chain_segments(h_end, a_end, carry_in, reverse):
    order = range(SUBLANES - 1, -1, -1) if reverse else range(SUBLANES)
    states = [None] * SUBLANES
    cur = carry_in
    for j in order:
        states[j] = cur
        cur = h_end[j:j + 1, :] + a_end[j:j + 1, :] * cur
    return states, cur


def _finish_scan(ends, hl_ref, ac_ref, carry_scr, c, reverse):
    states, carry = _chain_segments(ends[c][0], ends[c][1], carry_scr[c:c + 1, :], reverse)
    carry_scr[c:c + 1, :] = carry
    seg_state = jnp.concatenate(states, axis=0)[None]
    shape3 = (SEG, SUBLANES, LANES)
    fixed = hl_ref[c].reshape(shape3) + ac_ref[c].reshape(shape3) * seg_state
    return fixed.reshape(TOKEN_TILE, LANES)


def _lru_in_kernel(x_ref, xp_ref, xn_ref, g_ref, win_ref, cw_ref, cb_ref, gw_ref,
                   lam_ref, gate_out, hf_out, ab_out, bb_out,
                   rec_scr, xc_scr, sa_scr, sb_scr, hl_scr, ac_scr, carry_scr,
                   *, tiles_per_seq):
    tile = pl.program_id(0)
    t = TOKEN_TILE
    rows = t // CONV_STRIDE
    xa = _halo_tile(x_ref, xp_ref, xn_ref, tiles_per_seq, tile)
    xan = _rms_norm(xa, g_ref[...]).astype(_BF16)
    proj = jnp.dot(xan, win_ref[...], preferred_element_type=_F32)
    gate_out[...] = _gelu_tanh(proj[:t, :D_MODEL].astype(_BF16))
    for c in range(LRU_HEADS):
        _store_time_ordered(rec_scr, c, proj[:, D_MODEL + c * LANES:D_MODEL + (c + 1) * LANES])

    @pl.when(tile % tiles_per_seq == 0)
    def _():
        carry_scr[...] = jnp.zeros_like(carry_scr)

    half_decay = (-0.5 * LRU_C * LOG2_E) * _softplus(-lam_ref[...])

    for c in range(LRU_HEADS):
        lanes = slice(c * LANES, (c + 1) * LANES)
        cw = cw_ref[:, lanes]
        cb = cb_ref[:, lanes]
        for s in range(CONV_STRIDE):
            acc = cb
            for k in range(LRU_CONV):
                win = rec_scr[c, pl.ds(HALO - 2 + s + k, rows, stride=CONV_STRIDE), :]
                acc = acc + win * cw[k:k + 1, :]
            xc_scr[c % 2, pl.ds(s, rows, stride=CONV_STRIDE), :] = acc
        half_xc = xc_scr[c % 2]
        lhs = jnp.concatenate([half_xc.astype(_BF16), jnp.ones((t, LANES), _BF16)], axis=1)
        th = jnp.tanh(jnp.dot(lhs, gw_ref[c], preferred_element_type=_F32))
        for d in range(2):
            t_r = th[:, (2 * d) * LANES:(2 * d + 1) * LANES]
            t_i = th[:, (2 * d + 1) * LANES:(2 * d + 2) * LANES]
            hd = half_decay[d:d + 1, lanes]
            a = jnp.exp2(t_r * hd + hd)
            y = 1.0 - a * a
            mult = y * lax.rsqrt(jnp.maximum(y, TINY))
            b = (t_i + 1.0) * (mult * half_xc)
            if d == 0:
                _to_scan_order(sa_scr, c, a)
                _to_scan_order(sb_scr, c, b)
            else:
                _to_scan_order(ab_out, c, a)
                _to_scan_order(bb_out, c, b)

    ends = _local_scan(sa_scr, sb_scr, hl_scr, ac_scr, reverse=False)
    for c in range(LRU_HEADS):
        hf_out[c] = _finish_scan(ends, hl_scr, ac_scr, carry_scr, c, reverse=False).astype(_BF16)


def _lru_in(x, g, w_in, conv_w, conv_b, gate_w, lam, seq_len):
    n = x.shape[0]
    nt = n // TOKEN_TILE
    t = TOKEN_TILE
    tok = pl.BlockSpec((t, D_MODEL), lambda i: (i, 0))
    gate_shape = jax.ShapeDtypeStruct((n, D_MODEL), _BF16)
    hf_shape = jax.ShapeDtypeStruct((LRU_HEADS, n, LANES), _BF16)
    ab_shape = jax.ShapeDtypeStruct((LRU_HEADS, n, LANES), _F32)
    heads = pl.BlockSpec((LRU_HEADS, t, LANES), lambda i: (0, i, 0))
    head_scr = pltpu.VMEM((LRU_HEADS, t, LANES), _F32)
    return pl.pallas_call(
        functools.partial(_lru_in_kernel, tiles_per_seq=seq_len // t),
        grid=(nt,),
        in_specs=_halo_specs(n) + [
            _const_spec((1, D_MODEL)),
            _const_spec((D_MODEL, 2 * D_MODEL)),
            _const_spec((LRU_CONV, D_MODEL)),
            _const_spec((1, D_MODEL)),
            _const_spec((LRU_HEADS, 2 * LRU_BLOCK, 4 * LRU_BLOCK)),
            _const_spec((2, D_MODEL)),
        ],
        out_specs=[tok, heads, heads, heads],
        out_shape=[gate_shape, hf_shape, ab_shape, ab_shape],
        scratch_shapes=[
            pltpu.VMEM((LRU_HEADS, t + 2 * HALO, LANES), _F32),
            pltpu.VMEM((2, t, LANES), _F32),
            head_scr, head_scr, head_scr, head_scr,
            pltpu.VMEM((LRU_HEADS, LANES), _F32),
        ],
        compiler_params=pltpu.CompilerParams(
            dimension_semantics=("arbitrary",), vmem_limit_bytes=VMEM_LIMIT),
        name="lru_in",
    )(x, x, x, g, w_in, conv_w, conv_b, gate_w, lam)


def _lru_out_kernel(x_ref, gate_ref, hf_ref, ab_ref, bb_ref, wout_ref, out_ref,
                    y_scr, hl_scr, ac_scr, sum_scr, carry_scr, *, tiles_per_seq, n_tiles):
    tile = n_tiles - 1 - pl.program_id(0)

    @pl.when(tile % tiles_per_seq == tiles_per_seq - 1)
    def _():
        carry_scr[...] = jnp.zeros_like(carry_scr)

    ends = _local_scan(ab_ref, bb_ref, hl_scr, ac_scr, reverse=True)
    for c in range(LRU_HEADS):
        lanes = slice(c * LANES, (c + 1) * LANES)
        sum_scr[c] = (hf_ref[c].astype(_F32)
                      + _finish_scan(ends, hl_scr, ac_scr, carry_scr, c, reverse=True))
        for j in range(SUBLANES):
            rows = slice(j * SEG, (j + 1) * SEG)
            h = _segment_rows(sum_scr, c, j)
            y_scr[rows, lanes] = (h * gate_ref[rows, lanes].astype(_F32)).astype(_BF16)
    out_ref[...] = x_ref[...] + jnp.dot(y_scr[...], wout_ref[...],
                                        preferred_element_type=_F32)


def _lru_out(x, gate, hf, ab, bb, w_out, seq_len):
    n = x.shape[0]
    nt = n // TOKEN_TILE
    t = TOKEN_TILE
    tok = pl.BlockSpec((t, D_MODEL), lambda i: (nt - 1 - i, 0))
    heads = pl.BlockSpec((LRU_HEADS, t, LANES), lambda i: (0, nt - 1 - i, 0))
    head_scr = pltpu.VMEM((LRU_HEADS, t, LANES), _F32)
    return pl.pallas_call(
        functools.partial(_lru_out_kernel, tiles_per_seq=seq_len // t, n_tiles=nt),
        grid=(nt,),
        in_specs=[tok, tok, heads, heads, heads, _const_spec((D_MODEL, D_MODEL))],
        out_specs=tok,
        out_shape=jax.ShapeDtypeStruct((n, D_MODEL), _F32),
        scratch_shapes=[
            pltpu.VMEM((t, D_MODEL), _BF16),
            head_scr, head_scr, head_scr,
            pltpu.VMEM((LRU_HEADS, LANES), _F32),
        ],
        compiler_params=pltpu.CompilerParams(
            dimension_semantics=("arbitrary",), vmem_limit_bytes=VMEM_LIMIT),
        name="lru_out",
    )(x, gate, hf, ab, bb, w_out)


def _ffn_kernel(x_ref, xp_ref, xn_ref, g_ref, wup_ref, cw_ref, cb_ref, wdn_ref, gfin_ref,
                out_ref, h_scr, acc_scr, *, tiles_per_seq, final_norm):
    tile = pl.program_id(0)
    t = TOKEN_TILE
    rows = t // CONV_STRIDE
    nblk = FF_CHUNK // LANES
    n_chunks = D_FF // FF_CHUNK
    xa = _halo_tile(x_ref, xp_ref, xn_ref, tiles_per_seq, tile)
    xan = _rms_norm(xa, g_ref[...]).astype(_BF16)

    def up_project(ck):
        for half in range(2):
            col0 = half * D_FF + ck * FF_CHUNK
            h = jnp.dot(xan, wup_ref[:, col0:col0 + FF_CHUNK], preferred_element_type=_F32)
            for b in range(nblk):
                _store_time_ordered(h_scr, (ck % 2) * 2 * nblk + half * nblk + b,
                                    h[:, b * LANES:(b + 1) * LANES])

    def gated_conv(ck):
        blocks = []
        for b in range(nblk):
            pieces = []
            for s in range(CONV_STRIDE):
                conv = []
                for half in range(2):
                    col = half * D_FF + ck * FF_CHUNK + b * LANES
                    cw = cw_ref[:, col:col + LANES].astype(_BF16)
                    val = cb_ref[:, col:col + LANES].astype(_BF16)
                    for k in range(3):
                        win = h_scr[(ck % 2) * 2 * nblk + half * nblk + b,
                                    pl.ds(HALO - 1 + s + k, rows, stride=CONV_STRIDE), :]
                        val = val + win.astype(_BF16) * cw[k:k + 1, :]
                    conv.append(val)
                pieces.append(_gelu_tanh(conv[0]) * conv[1])
            blocks.append(jnp.concatenate(pieces, axis=0))
        return jnp.concatenate(blocks, axis=1)

    acc = jnp.zeros((t, D_MODEL), _F32)
    up_project(0)
    for ck in range(n_chunks):
        if ck + 1 < n_chunks:
            up_project(ck + 1)
        acc = acc + jnp.dot(gated_conv(ck), wdn_ref[ck * FF_CHUNK:(ck + 1) * FF_CHUNK, :],
                            preferred_element_type=_F32)
    for c in range(D_MODEL // LANES):
        for s in range(CONV_STRIDE):
            acc_scr[c, pl.ds(s, rows, stride=CONV_STRIDE), :] = (
                acc[s * rows:(s + 1) * rows, c * LANES:(c + 1) * LANES])
    y = x_ref[...] + jnp.concatenate(
        [acc_scr[c] for c in range(D_MODEL // LANES)], axis=1)
    if final_norm:
        y = _rms_norm(y, gfin_ref[...])
    out_ref[...] = y


def _conv_ffn(x, g, w_up, conv_w, conv_b, w_down, g_final, seq_len, final_norm):
    n = x.shape[0]
    t = TOKEN_TILE
    return pl.pallas_call(
        functools.partial(_ffn_kernel, tiles_per_seq=seq_len // t, final_norm=final_norm),
        grid=(n // t,),
        in_specs=_halo_specs(n) + [
            _const_spec((1, D_MODEL)),
            _const_spec((D_MODEL, 2 * D_FF)),
            _const_spec((3, 2 * D_FF)),
            _const_spec((1, 2 * D_FF)),
            _const_spec((D_FF, D_MODEL)),
            _const_spec((1, D_MODEL)),
        ],
        out_specs=pl.BlockSpec((t, D_MODEL), lambda i: (i, 0)),
        out_shape=jax.ShapeDtypeStruct((n, D_MODEL), _F32),
        scratch_shapes=[
            pltpu.VMEM((4 * FF_CHUNK // LANES, t + 2 * HALO, LANES), _F32),
            pltpu.VMEM((D_MODEL // LANES, t, LANES), _F32),
        ],
        compiler_params=pltpu.CompilerParams(
            dimension_semantics=("arbitrary",), vmem_limit_bytes=VMEM_LIMIT),
        name="conv_ffn",
    )(x, x, x, g, w_up, conv_w, conv_b, w_down, g_final)


def _qkv_kernel(x_ref, g_ref, w_ref, q_ref, k_ref, v_ref):
    xn = _rms_norm(x_ref[...], g_ref[...]).astype(_BF16)
    qkv = jnp.dot(xn, w_ref[...], preferred_element_type=_F32)
    q_ref[...] = (qkv[:, :D_MODEL] * (HEAD_DIM ** -0.5 * LOG2_E)).astype(_BF16)
    k_ref[...] = qkv[:, D_MODEL:2 * D_MODEL].astype(_BF16)
    v_ref[...] = qkv[:, 2 * D_MODEL:].astype(_BF16)


def _qkv(x, g, w_qkv):
    n = x.shape[0]
    t = TOKEN_TILE
    tok = pl.BlockSpec((t, D_MODEL), lambda i: (i, 0))
    out = jax.ShapeDtypeStruct((n, D_MODEL), _BF16)
    return pl.pallas_call(
        _qkv_kernel,
        grid=(n // t,),
        in_specs=[tok, _const_spec((1, D_MODEL)), _const_spec((D_MODEL, 3 * D_MODEL))],
        out_specs=[tok, tok, tok],
        out_shape=[out, out, out],
        compiler_params=pltpu.CompilerParams(
            dimension_semantics=("arbitrary",), vmem_limit_bytes=VMEM_LIMIT),
        name="qkv",
    )(x, g, w_qkv)


def _attn_kernel(q_ref, k_ref, v_ref, bias_ref, x_ref, wo_ref, out_ref, s_scr, o_scr,
                 *, blocks_per_seq, n_tiles):
    win = WIN_ROWS * GRID_W
    n_pairs = ATTN_HEADS // 2
    rows_per_seq = blocks_per_seq * ROW_BLOCK
    tile = pl.program_id(0)
    blk = tile % blocks_per_seq
    win_row0 = (jnp.clip(tile - 1, 0, n_tiles - 3) - (tile - blk)) * ROW_BLOCK
    lane = lax.broadcasted_iota(jnp.int32, (GRID_W, LANES), 1)
    low = lane < HEAD_DIM
    ones = jnp.ones((win, LANES), _BF16)

    def row_body(qr, carry):
        r = blk * ROW_BLOCK + qr
        rs = jnp.clip(r - WIN_ROWS // 2, 0, rows_per_seq - WIN_ROWS)
        shift = (WIN_ROWS - 1) - (r - rs)
        par = shift % 2
        m0 = shift // 2
        kstart = pl.multiple_of((rs - win_row0) * GRID_W, GRID_W)
        qrow = qr * GRID_W
        row_max = []
        for hp in range(n_pairs):
            lanes = slice(hp * LANES, (hp + 1) * LANES)
            q2 = q_ref[pl.ds(qrow, GRID_W), lanes]
            zero = jnp.zeros_like(q2)
            qs = jnp.concatenate([jnp.where(low, q2, zero), jnp.where(low, zero, q2)], axis=0)
            s = lax.dot_general(qs, k_ref[pl.ds(kstart, win), lanes], (((1,), (1,)), ((), ())),
                                preferred_element_type=_F32)
            bias = jnp.concatenate(
                [jnp.concatenate([bias_ref[par, 2 * hp + hh, m0 + m] for m in range(4)], axis=1)
                 for hh in range(2)], axis=0)
            s = s + bias
            s_scr[(qr % 2) * n_pairs + hp] = s
            row_max.append(jnp.max(s, axis=-1, keepdims=True))
        for hp in range(n_pairs):
            lanes = slice(hp * LANES, (hp + 1) * LANES)
            p = jnp.exp2(s_scr[(qr % 2) * n_pairs + hp] - row_max[hp]).astype(_BF16)
            v_ones = jnp.concatenate([v_ref[pl.ds(kstart, win), lanes], ones], axis=1)
            ov = jnp.dot(p, v_ones, preferred_element_type=_F32)
            o = ov[:, :LANES] * (1.0 / ov[:, LANES:])
            o2 = jnp.where(low, o[:GRID_W], o[GRID_W:])
            o_scr[pl.ds(qrow, GRID_W), lanes] = o2.astype(_BF16)
        return carry

    for qr in range(ROW_BLOCK):
        row_body(qr, 0)
    out_ref[...] = x_ref[...] + jnp.dot(o_scr[...], wo_ref[...], preferred_element_type=_F32)


def _attention(x, q, k, v, bias_tab, w_o, seq_len):
    n = x.shape[0]
    t = TOKEN_TILE
    nt = n // t
    tok = pl.BlockSpec((t, D_MODEL), lambda i: (i, 0))
    window = pl.BlockSpec((pl.Element(3 * t), pl.Element(D_MODEL)),
                          lambda i: (jnp.clip(i - 1, 0, nt - 3) * t, 0))
    return pl.pallas_call(
        functools.partial(_attn_kernel, blocks_per_seq=seq_len // t, n_tiles=nt),
        grid=(nt,),
        in_specs=[tok, window, window, _const_spec(bias_tab.shape), tok,
                  _const_spec((D_MODEL, D_MODEL))],
        out_specs=tok,
        out_shape=jax.ShapeDtypeStruct((n, D_MODEL), _F32),
        scratch_shapes=[
            pltpu.VMEM((ATTN_HEADS, 2 * GRID_W, WIN_ROWS * GRID_W), _F32),
            pltpu.VMEM((t, D_MODEL), _BF16),
        ],
        compiler_params=pltpu.CompilerParams(
            dimension_semantics=("arbitrary",), vmem_limit_bytes=VMEM_LIMIT),
        name="attention",
    )(q, k, v, bias_tab, x, w_o)


def _bias_table(rpb):
    col = jnp.arange(GRID_W, dtype=jnp.int32)
    col_start = jnp.clip(col - WIN_COLS // 2, 0, GRID_W - WIN_COLS)
    kc = jnp.arange(GRID_W, dtype=jnp.int32)
    valid = (kc[None, :] >= col_start[:, None]) & (kc[None, :] < col_start[:, None] + WIN_COLS)
    pad = GRID_W - WIN_COLS
    padded = jnp.pad(rpb.astype(_F32) * LOG2_E, ((0, 0), (0, 0), (pad, pad)))
    full = jnp.stack([padded[:, :, GRID_W - 1 - c:2 * GRID_W - 1 - c] for c in range(GRID_W)],
                     axis=2)
    full = jnp.where(valid[None, None], full, NEG_BIG)
    flat = jnp.transpose(full, (0, 2, 1, 3)).reshape(ATTN_HEADS, GRID_W, -1)
    flat = jnp.pad(flat, ((0, 0), (0, 0), (0, 64)), constant_values=NEG_BIG)
    tabs = []
    for par in range(2):
        sl = flat[:, :, 64 * par:64 * par + 7 * LANES]
        tabs.append(jnp.transpose(sl.reshape(ATTN_HEADS, GRID_W, 7, LANES), (0, 2, 1, 3)))
    return jnp.stack(tabs, axis=0)


def _trunk(x, seq_len, p):
    gate, hf, ab, bb = _lru_in(x, p["norm_mix"][0], p["w_in"], p["lru_conv_w"], p["lru_conv_b"],
                               p["gate_w"], p["lam"], seq_len)
    x = _lru_out(x, gate, hf, ab, bb, p["w_out"], seq_len)
    x = _conv_ffn(x, p["norm_ffn"][0], p["w_up"][0], p["ffn_conv_w"][0], p["ffn_conv_b"][0],
                  p["w_down"][0], p["norm_final"], seq_len, final_norm=False)
    q, k, v = _qkv(x, p["norm_mix"][1], p["w_qkv"])
    x = _attention(x, q, k, v, p["bias_tab"], p["w_o"], seq_len)
    x = _conv_ffn(x, p["norm_ffn"][1], p["w_up"][1], p["ffn_conv_w"][1], p["ffn_conv_b"][1],
                  p["w_down"][1], p["norm_final"], seq_len, final_norm=True)
    return x


def kernel(x_prompt, x_sample, norm_mix, norm_ffn, norm_final, lru_w_in, lru_conv_w, lru_conv_b, lru_gate_w, lru_gate_b, lru_lambda, lru_w_out, attn_w_qkv, attn_rpb, attn_w_o, ffn_w_up, ffn_conv_w, ffn_conv_b, ffn_w_down):
    gw = jnp.transpose(lru_gate_w[0], (2, 3, 0, 1, 4)).reshape(
        LRU_HEADS, LRU_BLOCK, 4 * LRU_BLOCK).astype(_BF16)
    gb = (0.5 * jnp.transpose(lru_gate_b[0], (2, 0, 1, 3))).reshape(
        LRU_HEADS, 1, 4 * LRU_BLOCK)
    gb_hi = gb.astype(_BF16)
    gb_lo = (gb - gb_hi.astype(_F32)).astype(_BF16)
    gw = jnp.concatenate(
        [gw, gb_hi, gb_lo, jnp.zeros((LRU_HEADS, LRU_BLOCK - 2, 4 * LRU_BLOCK), _BF16)], axis=1)
    p = {
        "norm_mix": [norm_mix[i][None, :] for i in range(2)],
        "norm_ffn": [norm_ffn[i][None, :] for i in range(2)],
        "norm_final": norm_final[None, :],
        "w_in": lru_w_in[0].astype(_BF16),
        "lru_conv_w": 0.5 * lru_conv_w[0],
        "lru_conv_b": 0.5 * lru_conv_b[0][None, :],
        "gate_w": gw,
        "lam": lru_lambda[0],
        "w_out": lru_w_out[0].astype(_BF16),
        "w_qkv": attn_w_qkv[0].astype(_BF16),
        "bias_tab": _bias_table(attn_rpb[0]),
        "w_o": attn_w_o[0].astype(_BF16),
        "w_up": [ffn_w_up[i].astype(_BF16) for i in range(2)],
        "ffn_conv_w": [ffn_conv_w[i] for i in range(2)],
        "ffn_conv_b": [ffn_conv_b[i][None, :] for i in range(2)],
        "w_down": [ffn_w_down[i].astype(_BF16) for i in range(2)],
    }
    outs = []
    for x in (x_prompt, x_sample):
        b, seq_len, d = x.shape
        y = _trunk(x.reshape(b * seq_len, d), seq_len, p)
        outs.append(y.reshape(b, seq_len, d))
    return tuple(outs)
```
